```python
import jax, jax.numpy as jnp
from jax import lax
import numpy as np

D_MODEL = 1024
BATCH = 2
SEQ = 8192
DEPTH = 4
DEC_BATCH = 128
DEC_SEQ = 1
PAST_LEN = 2048
PAGE_SIZE = 128

N_AB = (DEPTH + 1) // 2
N_C = DEPTH // 2
H_A = 4
DK_A = 128
DV_A = 128
F_A = H_A * DK_A
W_A = H_A * DV_A
H_B = 8
HD_B = 64
W_B = H_B * HD_B
H_C = 16
HD_C = 64
W_C = H_C * HD_C
AB_SIZES = (F_A, F_A, W_A, W_A, W_B, W_B, W_B, W_B)
C_SIZES = (W_C, W_C, W_C, W_C, H_C)
AB_COLS = sum(AB_SIZES)
C_COLS = sum(C_SIZES)
CHUNK_A = 64
Q_BLOCK = 128
EPS = 1e-6
NEG_BIG = -1e30

kernel_name = 'hybrid_hgrn2_stickbreak_fox_step'


def rms_norm(x, g):
    xf = x.astype(jnp.float32)
    y = xf * lax.rsqrt(jnp.mean(xf * xf, axis=-1, keepdims=True) + EPS)
    return (y * g.astype(jnp.float32)).astype(x.dtype)


def split_cols(h, sizes):
    parts, off = [], 0
    for n in sizes:
        parts.append(h[..., off:off + n])
        off += n
    return parts


def heads(a, n_heads):
    return a.reshape(a.shape[0], a.shape[1], n_heads, -1)


def gather_pages(cache, layer, page_table):
    rows = cache[layer, page_table]
    return rows.reshape(page_table.shape[0], -1, *cache.shape[3:])


def hgrn2_chunked(q, k, v, log_f, s0):
    B, T, H, _ = q.shape
    n = T // CHUNK_A

    def chunks(a):
        return a.reshape(B, n, CHUNK_A, H, a.shape[-1]).transpose(1, 0, 3, 2, 4)

    tri = jnp.tril(jnp.ones((CHUNK_A, CHUNK_A), bool))[:, :, None]

    def step(S, blk):
        qc, kc, vc, gc = blk
        G = jnp.cumsum(gc, axis=2)
        diff = G[:, :, :, None, :] - G[:, :, None, :, :]
        decay = jnp.where(tri, jnp.exp(jnp.where(tri, diff, 0.0)), 0.0)
        scores = jnp.einsum('bhtd,bhsd,bhtsd->bhts', qc, kc, decay)
        o = (jnp.einsum('bhts,bhsv->bhtv', scores, vc)
             + jnp.einsum('bhtd,bhdv->bhtv', qc * jnp.exp(G), S))
        G_end = G[:, :, -1:, :]
        S = (jnp.exp(G_end[:, :, 0, :])[..., None] * S
             + jnp.einsum('bhsd,bhsv->bhdv', kc * jnp.exp(G_end - G), vc))
        return S, o

    S, o = lax.scan(step, s0, tuple(chunks(a) for a in (q, k, v, log_f)))
    return o.transpose(1, 0, 3, 2, 4).reshape(B, T, H, -1), S


def hgrn2_recurrent(q, k, v, log_f, s0):
    def step(S, tok):
        qt, kt, vt, gt = tok
        S = jnp.exp(gt)[..., None] * S + kt[..., :, None] * vt[..., None, :]
        return S, jnp.einsum('bhd,bhdv->bhv', qt, S)

    S, o = lax.scan(step, s0, tuple(jnp.swapaxes(a, 0, 1) for a in (q, k, v, log_f)))
    return jnp.swapaxes(o, 0, 1), S


def segment_scores(q, ks, scale):
    return jnp.concatenate(
        [jnp.einsum('bthd,bshd->bhts', q, k, preferred_element_type=jnp.float32) for k in ks],
        axis=-1) * scale


def segment_mix(w, vs):
    out, off = None, 0
    for v in vs:
        n = v.shape[1]
        term = jnp.einsum('bhts,bshd->bthd', w[..., off:off + n].astype(v.dtype), v,
                          preferred_element_type=jnp.float32)
        out = term if out is None else out + term
        off += n
    return out


def stick_breaking_attend(q, ks, vs, q_pos, k_pos):
    z = segment_scores(q, ks, HD_B ** -0.5)
    causal = k_pos[None, :] < q_pos[:, None]
    log_keep = jnp.where(causal, jax.nn.log_sigmoid(-z), 0.0)
    after = lax.cumsum(log_keep, axis=3, reverse=True) - log_keep
    w = jnp.where(causal, jnp.exp(jnp.where(causal, jax.nn.log_sigmoid(z) + after, 0.0)), 0.0)
    return segment_mix(w, vs)


def stick_breaking_prompt(q, k, v):
    B, T, H, D = q.shape
    nb = T // Q_BLOCK
    qb = jnp.moveaxis(q.reshape(B, nb, Q_BLOCK, H, D), 1, 0)
    k_pos = jnp.arange(T)

    def block(args):
        i, q_i = args
        return stick_breaking_attend(q_i, (k,), (v,), i * Q_BLOCK + jnp.arange(Q_BLOCK), k_pos)

    o = lax.map(block, (jnp.arange(nb), qb))
    return jnp.moveaxis(o, 0, 1).reshape(B, T, H, D)


def fox_attend(q, ks, vs, c_q, c_k, q_pos, k_pos):
    logits = segment_scores(q, ks, HD_C ** -0.5)
    logits = logits + jnp.swapaxes(c_q, 1, 2)[:, :, :, None] - jnp.swapaxes(c_k, 1, 2)[:, :, None, :]
    logits = jnp.where(k_pos[None, :] <= q_pos[:, None], logits, NEG_BIG)
    return segment_mix(jax.nn.softmax(logits, axis=-1), vs)


def fox_prompt(q, k, v, log_f):
    B, T, H, D = q.shape
    nb = T // Q_BLOCK
    c = jnp.cumsum(log_f, axis=1)
    qb = jnp.moveaxis(q.reshape(B, nb, Q_BLOCK, H, D), 1, 0)
    cb = jnp.moveaxis(c.reshape(B, nb, Q_BLOCK, H), 1, 0)
    k_pos = jnp.arange(T)

    def block(args):
        i, q_i, c_i = args
        return fox_attend(q_i, (k,), (v,), c_i, c, i * Q_BLOCK + jnp.arange(Q_BLOCK), k_pos)

    o = lax.map(block, (jnp.arange(nb), qb, cb))
    return jnp.moveaxis(o, 0, 1).reshape(B, T, H, D)


def ab_layer(x, g_pre, w_in, w_out, lb, g_out, s0=None, past_k=None, past_v=None):
    B, T, _ = x.shape
    h = rms_norm(x, g_pre) @ w_in
    qa, fa, ia, ga, qb, kb, vb, gb = split_cols(h, AB_SIZES)
    lbh = lb.reshape(H_A, DK_A)
    za = heads(fa, H_A).astype(jnp.float32)
    log_f = jnp.log(lbh + (1.0 - lbh) * jax.nn.sigmoid(za))
    k_a = (1.0 - lbh) * jax.nn.sigmoid(-za)
    q_a = jax.nn.silu(heads(qa, H_A).astype(jnp.float32))
    v_a = heads(ia, H_A).astype(jnp.float32)
    q_b, k_b, v_b = heads(qb, H_B), heads(kb, H_B), heads(vb, H_B)
    if s0 is None:
        o_a, S = hgrn2_chunked(q_a, k_a, v_a, log_f, jnp.zeros((B, H_A, DK_A, DV_A), jnp.float32))
        o_b = stick_breaking_prompt(q_b, k_b, v_b)
    else:
        o_a, S = hgrn2_recurrent(q_a, k_a, v_a, log_f, s0.astype(jnp.float32))
        P = past_k.shape[1]
        o_b = stick_breaking_attend(q_b, (past_k, k_b), (past_v, v_b), P + jnp.arange(T), jnp.arange(P + T))
    o_a = rms_norm(o_a, g_out).reshape(B, T, W_A) * jax.nn.silu(ga.astype(jnp.float32))
    o_b = o_b.reshape(B, T, W_B) * jax.nn.silu(gb.astype(jnp.float32))
    y = x + jnp.concatenate([o_a, o_b], axis=-1).astype(x.dtype) @ w_out
    return y, S, k_b, v_b


def c_layer(x, g_pre, w_in, b_f, w_out, g_q, g_k, past_k=None, past_v=None, past_logf=None):
    B, T, _ = x.shape
    h = rms_norm(x, g_pre) @ w_in
    qc, kc, vc, gc, fc = split_cols(h, C_SIZES)
    q = rms_norm(heads(qc, H_C), g_q)
    k = rms_norm(heads(kc, H_C), g_k)
    v = heads(vc, H_C)
    log_f = jax.nn.log_sigmoid((fc + b_f).astype(jnp.float32))
    if past_k is None:
        o = fox_prompt(q, k, v, log_f)
    else:
        P = past_k.shape[1]
        c = jnp.cumsum(jnp.concatenate([past_logf.astype(jnp.float32), log_f], axis=1), axis=1)
        o = fox_attend(q, (past_k, k), (past_v, v), c[:, P:], c, P + jnp.arange(T), jnp.arange(P + T))
    o = o.reshape(B, T, W_C) * jax.nn.silu(gc.astype(jnp.float32))
    y = x + o.astype(x.dtype) @ w_out
    return y, k, v, log_f


def setup_inputs(seed: int = 0) -> dict:
    key = jax.random.key(seed)
    ks = jax.random.split(key, 20)
    f32 = jnp.float32
    n_pages = PAST_LEN // PAGE_SIZE
    used = DEC_BATCH * n_pages
    n_pool = used + max(1, used // 4)

    def nrm(k, shape, s=1.0):
        return s * jax.random.normal(k, shape, f32)

    x_prompt = nrm(ks[0], (BATCH, SEQ, D_MODEL))
    x_sample = nrm(ks[1], (DEC_BATCH, DEC_SEQ, D_MODEL))
    state_hgrn = nrm(ks[2], (N_AB, DEC_BATCH, H_A, DK_A, DV_A), 0.5)
    cache_sb_k = nrm(ks[3], (N_AB, n_pool, PAGE_SIZE, H_B, HD_B))
    cache_sb_v = nrm(ks[4], (N_AB, n_pool, PAGE_SIZE, H_B, HD_B))
    cache_fox_k = nrm(ks[5], (N_C, n_pool, PAGE_SIZE, H_C, HD_C))
    cache_fox_v = nrm(ks[6], (N_C, n_pool, PAGE_SIZE, H_C, HD_C))
    cache_fox_logf = jax.nn.log_sigmoid(2.0 + nrm(ks[7], (N_C, n_pool, PAGE_SIZE, H_C)))
    page_table = jax.random.permutation(ks[8], n_pool)[:used].reshape(DEC_BATCH, n_pages).astype(jnp.int32)
    norm_g = 1.0 + nrm(ks[9], (DEPTH, D_MODEL), 0.02)
    w_in_ab = nrm(ks[10], (N_AB, D_MODEL, AB_COLS), D_MODEL ** -0.5)
    w_out_ab = nrm(ks[11], (N_AB, W_A + W_B, D_MODEL), (W_A + W_B) ** -0.5)
    lb_param = nrm(ks[12], (N_AB, F_A))
    out_norm_a = 1.0 + nrm(ks[13], (N_AB, DV_A), 0.02)
    w_in_c = nrm(ks[14], (N_C, D_MODEL, C_COLS), D_MODEL ** -0.5)
    b_f_c = jnp.linspace(1.0, 6.0, H_C, dtype=f32) + nrm(ks[15], (N_C, H_C), 0.1)
    w_out_c = nrm(ks[16], (N_C, W_C, D_MODEL), W_C ** -0.5)
    qk_norm_q = 1.0 + nrm(ks[17], (N_C, HD_C), 0.02)
    qk_norm_k = 1.0 + nrm(ks[18], (N_C, HD_C), 0.02)
    return {'x_prompt': x_prompt, 'x_sample': x_sample, 'state_hgrn': state_hgrn,
            'cache_sb_k': cache_sb_k, 'cache_sb_v': cache_sb_v,
            'cache_fox_k': cache_fox_k, 'cache_fox_v': cache_fox_v, 'cache_fox_logf': cache_fox_logf,
            'page_table': page_table, 'norm_g': norm_g, 'w_in_ab': w_in_ab, 'w_out_ab': w_out_ab,
            'lb_param': lb_param, 'out_norm_a': out_norm_a, 'w_in_c': w_in_c, 'b_f_c': b_f_c,
            'w_out_c': w_out_c, 'qk_norm_q': qk_norm_q, 'qk_norm_k': qk_norm_k}


def reference(x_prompt, x_sample, state_hgrn, cache_sb_k, cache_sb_v, cache_fox_k, cache_fox_v,
              cache_fox_logf, page_table, norm_g, w_in_ab, w_out_ab, lb_param, out_norm_a,
              w_in_c, b_f_c, w_out_c, qk_norm_q, qk_norm_k):
    p = jax.nn.softmax(lb_param.astype(jnp.float32), axis=0)
    lower_bounds = jnp.cumsum(p, axis=0) - p[0]
    yp, ys = x_prompt, x_sample
    hgrn_p, hgrn_s, sbk_p, sbv_p, sbk_s, sbv_s = [], [], [], [], [], []
    fk_p, fv_p, fl_p, fk_s, fv_s, fl_s = [], [], [], [], [], []
    for l in range(DEPTH):
        j = l // 2
        if l % 2 == 0:
            yp, S, k, v = ab_layer(yp, norm_g[l], w_in_ab[j], w_out_ab[j], lower_bounds[j], out_norm_a[j])
            hgrn_p.append(S)
            sbk_p.append(k)
            sbv_p.append(v)
            ys, S, k, v = ab_layer(ys, norm_g[l], w_in_ab[j], w_out_ab[j], lower_bounds[j], out_norm_a[j],
                                   state_hgrn[j], gather_pages(cache_sb_k, j, page_table),
                                   gather_pages(cache_sb_v, j, page_table))
            hgrn_s.append(S)
            sbk_s.append(k)
            sbv_s.append(v)
        else:
            yp, k, v, lf = c_layer(yp, norm_g[l], w_in_c[j], b_f_c[j], w_out_c[j], qk_norm_q[j], qk_norm_k[j])
            fk_p.append(k)
            fv_p.append(v)
            fl_p.append(lf)
            ys, k, v, lf = c_layer(ys, norm_g[l], w_in_c[j], b_f_c[j], w_out_c[j], qk_norm_q[j], qk_norm_k[j],
                                   gather_pages(cache_fox_k, j, page_table),
                                   gather_pages(cache_fox_v, j, page_table),
                                   gather_pages(cache_fox_logf, j, page_table))
            fk_s.append(k)
            fv_s.append(v)
            fl_s.append(lf)
    return (yp, ys, jnp.stack(hgrn_p), jnp.stack(hgrn_s), jnp.stack(sbk_p), jnp.stack(sbv_p),
            jnp.stack(sbk_s), jnp.stack(sbv_s), jnp.stack(fk_p), jnp.stack(fv_p), jnp.stack(fl_p),
            jnp.stack(fk_s), jnp.stack(fv_s), jnp.stack(fl_s))
```

```python
import functools

import jax
import jax.numpy as jnp
from jax import lax
from jax.experimental import pallas as pl
from jax.experimental.pallas import tpu as pltpu

F32 = jnp.float32
BF16 = jnp.bfloat16
EPS = 1e-6
NEG_BIG = -1e30
LANES = 128
MIB = 1024 * 1024
NT_DIMS = (((1,), (1,)), ((), ()))


def _cparams(semantics, vmem_mib):
    return pltpu.CompilerParams(dimension_semantics=semantics, vmem_limit_bytes=vmem_mib * MIB)


def _dot(a, b):
    return jnp.dot(a, b, preferred_element_type=F32)


def _dot_nt(a, b):
    return lax.dot_general(a, b, NT_DIMS, preferred_element_type=F32)


def _split2(x):
    hi = x.astype(BF16)
    lo = (x - hi.astype(F32)).astype(BF16)
    return hi, lo


def _split3(x):
    hi = x.astype(BF16)
    r = x - hi.astype(F32)
    mid = r.astype(BF16)
    lo = (r - mid.astype(F32)).astype(BF16)
    return hi, mid, lo


def _dot_left_exact(mat01, x):
    hi, mid, lo = _split3(x)
    return _dot(mat01, hi) + _dot(mat01, mid) + _dot(mat01, lo)


def _sigmoid(x):
    return 1.0 / (1.0 + jnp.exp(-x))


def _softplus(x):
    return jnp.maximum(x, 0.0) + jnp.log(1.0 + jnp.exp(-jnp.abs(x)))


def _rms(x, g):
    return x * lax.rsqrt(jnp.mean(x * x, axis=-1, keepdims=True) + EPS) * g


def _full(shape):
    nd = len(shape)
    return pl.BlockSpec(shape, lambda *_: (0,) * nd)


def _inproj_ab_kernel(x_ref, g_ref, w_ref, lb_ref,
                      qa_ref, ka_ref, lf_ref, va_ref, ga_ref, gb_ref, kbo_ref, vbo_ref,
                      qb_ref, kb_ref, vbt_ref, *, fa, wa, wb, sb_scale):
    xn = _rms(x_ref[...], g_ref[...]).astype(BF16)
    offs = [0]

    def proj(n):
        lo = offs[0]
        offs[0] = lo + n
        return _dot(xn, w_ref[:, lo:lo + n])

    qa = proj(fa)
    za = proj(fa)
    ia = proj(wa)
    ga = proj(wa)
    qb = proj(wb)
    kb = proj(wb)
    vb = proj(wb)
    gb = proj(wb)
    lb = lb_ref[...]
    lf_ref[...] = jnp.log(lb + (1.0 - lb) * _sigmoid(za))
    ka_ref[...] = (1.0 - lb) * _sigmoid(-za)
    qa_ref[...] = qa * _sigmoid(qa)
    va_ref[...] = ia
    ga_ref[...] = ga * _sigmoid(ga)
    gb_ref[...] = gb * _sigmoid(gb)
    kbo_ref[...] = kb
    vbo_ref[...] = vb
    qb_ref[...] = (qb * sb_scale).astype(BF16)
    kb_ref[...] = kb.astype(BF16)
    vbt_ref[...] = vb.T.astype(BF16)


def _inproj_ab(x, g, w_bf16, lb, *, fa, wa, wb, hd_b, tm):
    B, T, D = x.shape
    nt = T // tm
    cols = w_bf16.shape[1]
    row = lambda n: pl.BlockSpec((None, tm, n), lambda b, t: (b, t, 0))
    f32o = lambda n: jax.ShapeDtypeStruct((B, T, n), F32)
    out_shape = (f32o(fa), f32o(fa), f32o(fa), f32o(wa), f32o(wa), f32o(wb), f32o(wb), f32o(wb),
                 jax.ShapeDtypeStruct((B, T, wb), BF16), jax.ShapeDtypeStruct((B, T, wb), BF16),
                 jax.ShapeDtypeStruct((B, nt, wb, tm), BF16))
    out_specs = (row(fa), row(fa), row(fa), row(wa), row(wa), row(wb), row(wb), row(wb), row(wb), row(wb),
                 pl.BlockSpec((None, None, wb, tm), lambda b, t: (b, t, 0, 0)))
    kern = functools.partial(_inproj_ab_kernel, fa=fa, wa=wa, wb=wb, sb_scale=float(hd_b ** -0.5))
    return pl.pallas_call(
        kern, grid=(B, nt),
        in_specs=[row(D), _full((1, D)), _full((D, cols)), _full((1, fa))],
        out_specs=out_specs, out_shape=out_shape,
        compiler_params=_cparams(("parallel", "parallel"), 48), name="inproj_ab",
    )(x, g, w_bf16, lb)


def _outproj_ab_kernel(x_ref, oa_ref, ga_ref, ob_ref, gb_ref, gout_ref, w_ref, y_ref, *, nh, dv, ob_transposed):
    oa = oa_ref[...]
    gout = gout_ref[...]
    parts = [_rms(oa[:, h * dv:(h + 1) * dv], gout) for h in range(nh)]
    oa_n = jnp.concatenate(parts, axis=1) * ga_ref[...]
    ob = ob_ref[...]
    if ob_transposed:
        ob = ob.T
    ob = ob * gb_ref[...]
    cat = jnp.concatenate([oa_n, ob], axis=1).astype(BF16)
    y_ref[...] = x_ref[...] + _dot(cat, w_ref[...])


def _outproj_ab(x, oa, ga, ob, gb, gout, w_bf16, *, nh, dv, tm, ob_transposed):
    B, T, D = x.shape
    nt = T // tm
    wa = oa.shape[-1]
    wb = gb.shape[-1]
    row = lambda n: pl.BlockSpec((None, tm, n), lambda b, t: (b, t, 0))
    if ob_transposed:
        ob_spec = pl.BlockSpec((None, wb, tm), lambda b, t: (b, 0, t))
    else:
        ob_spec = row(wb)
    kern = functools.partial(_outproj_ab_kernel, nh=nh, dv=dv, ob_transposed=ob_transposed)
    return pl.pallas_call(
        kern, grid=(B, nt),
        in_specs=[row(D), row(wa), row(wa), ob_spec, row(wb), _full((1, dv)), _full((wa + wb, D))],
        out_specs=row(D), out_shape=jax.ShapeDtypeStruct((B, T, D), F32),
        compiler_params=_cparams(("parallel", "parallel"), 40), name="outproj_ab",
    )(x, oa, ga, ob, gb, gout, w_bf16)


def _inproj_c_kernel(x_ref, g_ref, w_ref, wf_ref, bf_ref, gq_ref, gk_ref, bd_ref, *out_refs,
                     wc, hd, nh, nb, fox_scale, decode):
    xn = _rms(x_ref[...], g_ref[...]).astype(BF16)
    bd = bd_ref[...]
    inv_hd = 1.0 / hd

    def normed(base, gfull_ref):
        outs = []
        for c in range(wc // nb):
            y = _dot(xn, w_ref[:, base + c * nb: base + (c + 1) * nb])
            hi, lo = _split2(y * y)
            ms = (_dot(hi, bd) + _dot(lo, bd)) * inv_hd
            outs.append(y * lax.rsqrt(ms + EPS) * gfull_ref[:, c * nb:(c + 1) * nb])
        return jnp.concatenate(outs, axis=1)

    qn = normed(0, gq_ref)
    kn = normed(wc, gk_ref)
    v = _dot(xn, w_ref[:, 2 * wc:3 * wc])
    gt = _dot(xn, w_ref[:, 3 * wc:4 * wc])
    f = _dot(xn, wf_ref[...])[:, :nh] + bf_ref[...]
    logf = -_softplus(-f)
    if decode:
        qn_ref, k_ref, v_ref, gate_ref, lf_ref = out_refs
        qn_ref[...] = qn
    else:
        qp_ref, k_ref, v_ref, gate_ref, lf_ref, vt_ref = out_refs
        lane = lax.broadcasted_iota(jnp.int32, (qn.shape[0], LANES), 1)
        ones_cols = jnp.where((lane >= hd) & (lane < hd + 3), 1.0, 0.0)
        for h in range(nh):
            pair = qn[:, (h // 2) * LANES:(h // 2 + 1) * LANES]
            if h % 2 == 1:
                pair = pltpu.roll(pair, hd, axis=1)
            qp_ref[h] = jnp.where(lane < hd, pair * fox_scale, ones_cols).astype(BF16)
        vt_ref[...] = v.T.astype(BF16)
    k_ref[...] = kn
    v_ref[...] = v
    gate_ref[...] = gt * _sigmoid(gt)
    lf_ref[...] = logf


def _inproj_c(x, g, w_bf16, wf_bf16, bf, gq_full, gk_full, bd, *, wc, hd, nh, tm, decode):
    B, T, D = x.shape
    nt = T // tm
    nb = bd.shape[0]
    row = lambda n: pl.BlockSpec((None, tm, n), lambda b, t: (b, t, 0))
    f32o = lambda n: jax.ShapeDtypeStruct((B, T, n), F32)
    common_shapes = (f32o(wc), f32o(wc), f32o(wc), f32o(nh))
    common_specs = (row(wc), row(wc), row(wc), row(nh))
    if decode:
        out_shape = (f32o(wc),) + common_shapes
        out_specs = (row(wc),) + common_specs
    else:
        out_shape = ((jax.ShapeDtypeStruct((B, nh, T, LANES), BF16),) + common_shapes
                     + (jax.ShapeDtypeStruct((B, nt, wc, tm), BF16),))
        out_specs = ((pl.BlockSpec((None, nh, tm, LANES), lambda b, t: (b, 0, t, 0)),) + common_specs
                     + (pl.BlockSpec((None, None, wc, tm), lambda b, t: (b, t, 0, 0)),))
    kern = functools.partial(_inproj_c_kernel, wc=wc, hd=hd, nh=nh, nb=nb, fox_scale=float(hd ** -0.5), decode=decode)
    return pl.pallas_call(
        kern, grid=(B, nt),
        in_specs=[row(D), _full((1, D)), _full((D, 4 * wc)), _full((D, LANES)), _full((1, nh)),
                  _full((1, wc)), _full((1, wc)), _full((nb, nb))],
        out_specs=out_specs, out_shape=out_shape,
        compiler_params=_cparams(("parallel", "parallel"), 48), name="inproj_c",
    )(x, g, w_bf16, wf_bf16, bf, gq_full, gk_full, bd)


def _outproj_c_kernel(x_ref, o_ref, gate_ref, w_ref, y_ref, *, o_transposed):
    o = o_ref[...]
    if o_transposed:
        o = o.T
    y_ref[...] = x_ref[...] + _dot((o * gate_ref[...]).astype(BF16), w_ref[...])


def _outproj_c(x, o, gate, w_bf16, *, tm, o_transposed):
    B, T, D = x.shape
    nt = T // tm
    wc = gate.shape[-1]
    row = lambda n: pl.BlockSpec((None, tm, n), lambda b, t: (b, t, 0))
    o_spec = pl.BlockSpec((None, wc, tm), lambda b, t: (b, 0, t)) if o_transposed else row(wc)
    kern = functools.partial(_outproj_c_kernel, o_transposed=o_transposed)
    return pl.pallas_call(
        kern, grid=(B, nt),
        in_specs=[row(D), o_spec, row(wc), _full((wc, D))],
        out_specs=row(D), out_shape=jax.ShapeDtypeStruct((B, T, D), F32),
        compiler_params=_cparams(("parallel", "parallel"), 40), name="outproj_c",
    )(x, o, gate, w_bf16)


HGRN_CHUNK = 128
HGRN_SUB = 16
HGRN_EXP_CLAMP = 60.0


def _hgrn_prompt_kernel(q_ref, k_ref, v_ref, g_ref, ltri_ref, o_ref, s_ref, st_ref, *, nh, dk, tc):
    t = pl.program_id(1)

    @pl.when(t == 0)
    def _():
        st_ref[...] = jnp.zeros_like(st_ref)

    C, SB = HGRN_CHUNK, HGRN_SUB
    ltri = ltri_ref[...]

    def chunk(c, carry):
        r0 = pl.multiple_of(c * C, C)
        for h in range(nh):
            sl = slice(h * dk, (h + 1) * dk)
            q = q_ref[pl.ds(r0, C), sl]
            k = k_ref[pl.ds(r0, C), sl]
            v = v_ref[pl.ds(r0, C), sl]
            g = g_ref[pl.ds(r0, C), sl]
            G = _dot_left_exact(ltri, g)
            st = st_ref[h]
            o = _dot_nt((q * jnp.exp(G)).astype(BF16), st.astype(BF16))
            vb = v.astype(BF16)
            outs = []
            for i in range(C // SB):
                a, n = i * SB, (i + 1) * SB
                r = G[a - 1:a, :] if i > 0 else jnp.zeros((1, dk), F32)
                qt = (q[a:n] * jnp.exp(G[a:n] - r)).astype(BF16)
                kt = (k[:n] * jnp.exp(jnp.minimum(r - G[:n], HGRN_EXP_CLAMP))).astype(BF16)
                sc = _dot_nt(qt, kt)
                row = lax.broadcasted_iota(jnp.int32, (SB, n), 0) + a
                col = lax.broadcasted_iota(jnp.int32, (SB, n), 1)
                sc = jnp.where(col <= row, sc, 0.0)
                outs.append(_dot(sc.astype(BF16), vb[:n]))
            o_ref[pl.ds(r0, C), sl] = o + jnp.concatenate(outs, axis=0)
            gend = G[C - 1:C, :]
            kbar = (k * jnp.exp(gend - G)).astype(BF16)
            st_ref[h] = st * jnp.exp(gend) + _dot(v.T.astype(BF16), kbar)
        return carry

    lax.fori_loop(0, tc // C, chunk, 0)

    @pl.when(t == pl.num_programs(1) - 1)
    def _():
        for h in range(nh):
            s_ref[h] = st_ref[h].T


def _hgrn_prompt(q, k, v, g, ltri, *, nh, dk, tc):
    B, T, W = q.shape
    nt = T // tc
    row = pl.BlockSpec((None, tc, W), lambda b, t: (b, t, 0))
    kern = functools.partial(_hgrn_prompt_kernel, nh=nh, dk=dk, tc=tc)
    return pl.pallas_call(
        kern, grid=(B, nt),
        in_specs=[row, row, row, row, _full(ltri.shape)],
        out_specs=(row, pl.BlockSpec((None, nh, dk, dk), lambda b, t: (b, 0, 0, 0))),
        out_shape=(jax.ShapeDtypeStruct((B, T, W), F32), jax.ShapeDtypeStruct((B, nh, dk, dk), F32)),
        scratch_shapes=[pltpu.VMEM((nh, dk, dk), F32)],
        compiler_params=_cparams(("parallel", "arbitrary"), 32), name="hgrn_prompt",
    )(q, k, v, g, ltri)


HGRN_STEP_SEQS = 8


def _hgrn_step_kernel(q_ref, k_ref, v_ref, g_ref, s_ref, o_ref, so_ref, *, nh, dk):
    nb = HGRN_STEP_SEQS
    pad = jnp.zeros((dk - nb, dk), F32)

    def columns(x8):
        return jnp.concatenate([x8, pad], axis=0).T

    for h in range(nh):
        sl = slice(h * dk, (h + 1) * dk)
        egc = columns(jnp.exp(g_ref[:, sl]))
        kc = columns(k_ref[:, sl])
        qc = columns(q_ref[:, sl])
        for j in range(nb):
            s_new = egc[:, j:j + 1] * s_ref[j, h] + kc[:, j:j + 1] * v_ref[j:j + 1, sl]
            so_ref[j, h] = s_new
            o_ref[j:j + 1, sl] = jnp.sum(qc[:, j:j + 1] * s_new, axis=0, keepdims=True)


def _hgrn_step(q, k, v, g, s0, *, nh, dk):
    Bd, W = q.shape
    nb = HGRN_STEP_SEQS
    row = pl.BlockSpec((nb, W), lambda b: (b, 0))
    st = pl.BlockSpec((nb, nh, dk, dk), lambda b: (b, 0, 0, 0))
    kern = functools.partial(_hgrn_step_kernel, nh=nh, dk=dk)
    return pl.pallas_call(
        kern, grid=(Bd // nb,),
        in_specs=[row, row, row, row, st],
        out_specs=(row, st),
        out_shape=(jax.ShapeDtypeStruct((Bd, W), F32), jax.ShapeDtypeStruct(s0.shape, F32)),
        compiler_params=_cparams(("parallel",), 32), name="hgrn_step",
    )(q, k, v, g, s0)


def _sb_prompt_kernel(q_ref, k_ref, vt_ref, u_ref, o_ref, *, tq, hd):
    h = pl.program_id(1)
    i = pl.program_id(2)
    lane = lax.broadcasted_iota(jnp.int32, (tq, LANES), 1)
    q2 = q_ref[...]
    q = jnp.where((lane // hd) == (h % 2), q2, jnp.zeros_like(q2))
    u = u_ref[...]
    srow = lax.broadcasted_iota(jnp.int32, (tq, tq), 0)
    tcol = lax.broadcasted_iota(jnp.int32, (tq, tq), 1)
    causal = srow < tcol

    def block(j, carry, acc, masked):
        kb = k_ref[pl.ds(pl.multiple_of(j * tq, tq), tq), :]
        z = _dot_nt(kb, q)
        sp = _softplus(z)
        if masked:
            sp = jnp.where(causal, sp, 0.0)
        hi, lo = _split2(sp)
        later = _dot(u, hi) + _dot(u, lo)
        w = jnp.exp(z - sp - later - carry)
        if masked:
            w = jnp.where(causal, w, 0.0)
        acc = acc + _dot(vt_ref[j], w.astype(BF16))
        carry = carry + jnp.sum(sp, axis=0, keepdims=True)
        return carry, acc

    carry, acc = block(i, jnp.zeros((1, tq), F32), jnp.zeros((hd, tq), F32), True)

    def body(jj, ca):
        return block(i - 1 - jj, ca[0], ca[1], False)

    carry, acc = lax.fori_loop(0, i, body, (carry, acc))
    o_ref[...] = acc


def _sb_prompt(qb, kb, vbt, ustrict, *, nh, hd, tq):
    B, T, W = qb.shape
    nt = T // tq
    kern = functools.partial(_sb_prompt_kernel, tq=tq, hd=hd)
    return pl.pallas_call(
        kern, grid=(B, nh, nt),
        in_specs=[pl.BlockSpec((None, tq, LANES), lambda b, h, i: (b, i, h // 2)),
                  pl.BlockSpec((None, T, LANES), lambda b, h, i: (b, 0, h // 2)),
                  pl.BlockSpec((None, nt, None, hd, tq), lambda b, h, i: (b, 0, h, 0, 0)),
                  _full((tq, tq))],
        out_specs=pl.BlockSpec((None, hd, tq), lambda b, h, i: (b, h, i)),
        out_shape=jax.ShapeDtypeStruct((B, W, T), F32),
        compiler_params=_cparams(("parallel", "parallel", "arbitrary"), 40), name="sb_prompt",
    )(qb, kb, vbt, ustrict)


def _fox_prep_kernel(kn_ref, lf_ref, ltri_ref, e_ref, kp_ref, carry_ref, *, nh, hd):
    t = pl.program_id(1)

    @pl.when(t == 0)
    def _():
        carry_ref[...] = jnp.zeros_like(carry_ref)

    c = carry_ref[...] + _dot_left_exact(ltri_ref[...], lf_ref[...])
    tm = c.shape[0]
    carry_ref[...] = c[tm - 1:tm, :]
    hi, mid, lo = _split3(-c)
    aug = _dot(hi, e_ref[0]) + _dot(mid, e_ref[1]) + _dot(lo, e_ref[2])
    kn = kn_ref[...]
    lane = lax.broadcasted_iota(jnp.int32, (tm, LANES), 1)
    for h in range(nh):
        pair = kn[:, (h // 2) * LANES:(h // 2 + 1) * LANES]
        if h % 2 == 1:
            pair = pltpu.roll(pair, hd, axis=1)
        kp_ref[h] = jnp.where(lane < hd, pair, aug[:, h * LANES:(h + 1) * LANES]).astype(BF16)


def _fox_prep(kn, logf, ltri, emat, *, nh, hd, tm):
    B, T, W = kn.shape
    nt = T // tm
    kern = functools.partial(_fox_prep_kernel, nh=nh, hd=hd)
    return pl.pallas_call(
        kern, grid=(B, nt),
        in_specs=[pl.BlockSpec((None, tm, W), lambda b, t: (b, t, 0)),
                  pl.BlockSpec((None, tm, nh), lambda b, t: (b, t, 0)),
                  _full(ltri.shape), _full(emat.shape)],
        out_specs=pl.BlockSpec((None, nh, tm, LANES), lambda b, t: (b, 0, t, 0)),
        out_shape=jax.ShapeDtypeStruct((B, nh, T, LANES), BF16),
        scratch_shapes=[pltpu.VMEM((1, nh), F32)],
        compiler_params=_cparams(("parallel", "arbitrary"), 40), name="fox_prep",
    )(kn, logf, ltri, emat)


def _fox_prompt_kernel(q_ref, k_ref, vt_ref, o_ref, *, tq, hd):
    i = pl.program_id(2)
    q = q_ref[...]
    srow = lax.broadcasted_iota(jnp.int32, (tq, tq), 0)
    tcol = lax.broadcasted_iota(jnp.int32, (tq, tq), 1)
    causal = srow <= tcol

    def block(j, m, l, acc, masked):
        s = _dot_nt(k_ref[j], q)
        if masked:
            s = jnp.where(causal, s, NEG_BIG)
        m_new = jnp.maximum(m, jnp.max(s, axis=0, keepdims=True))
        alpha = jnp.exp(m - m_new)
        p = jnp.exp(s - m_new)
        l = alpha * l + jnp.sum(p, axis=0, keepdims=True)
        acc = alpha * acc + _dot(vt_ref[j], p.astype(BF16))
        return m_new, l, acc

    m, l, acc = block(i, jnp.full((1, tq), NEG_BIG, F32), jnp.zeros((1, tq), F32), jnp.zeros((hd, tq), F32), True)

    def body(jj, c):
        return block(i - 1 - jj, c[0], c[1], c[2], False)

    m, l, acc = lax.fori_loop(0, i, body, (m, l, acc))
    o_ref[...] = acc / l


def _fox_prompt(qp, kp, vt, *, nh, hd, tq):
    B, _, T, _ = qp.shape
    nt = T // tq
    kp5 = kp.reshape(B, nh, nt, tq, LANES)
    kern = functools.partial(_fox_prompt_kernel, tq=tq, hd=hd)
    return pl.pallas_call(
        kern, grid=(B, nh, nt),
        in_specs=[pl.BlockSpec((None, None, tq, LANES), lambda b, h, i: (b, h, i, 0)),
                  pl.BlockSpec((None, None, nt, tq, LANES), lambda b, h, i: (b, h, 0, 0, 0)),
                  pl.BlockSpec((None, nt, None, hd, tq), lambda b, h, i: (b, 0, h, 0, 0))],
        out_specs=pl.BlockSpec((None, hd, tq), lambda b, h, i: (b, h, i)),
        out_shape=jax.ShapeDtypeStruct((B, nh * hd, T), F32),
        compiler_params=_cparams(("parallel", "parallel", "arbitrary"), 40), name="fox_prompt",
    )(qp, kp5, vt)


DEC_PAGES_PER_STEP = 8


def _head_mask(nh):
    sub = lax.broadcasted_iota(jnp.int32, (nh, LANES), 0)
    lane = lax.broadcasted_iota(jnp.int32, (nh, LANES), 1)
    return (lane % nh) == sub


def _dec_scores_kernel(pt_ref, q_ref, *refs, nh, npp):
    k_refs, z_ref = refs[:npp], refs[npp]
    qb = q_ref[...].astype(BF16)
    hmask = _head_mask(nh)
    rowid = lax.broadcasted_iota(jnp.int32, (nh, LANES), 0)
    for p in range(npp):
        kf = k_refs[p][...].astype(BF16)
        zall = _dot_nt(qb, kf)
        zd = jnp.zeros((nh, LANES), F32)
        for r in range(nh):
            picked = jnp.sum(jnp.where(hmask, zall[:, r * LANES:(r + 1) * LANES], 0.0), axis=0, keepdims=True)
            zd = jnp.where(rowid == r, picked, zd)
        z_ref[p * nh:(p + 1) * nh, :] = zd


def _page_specs(layer, rows, hd, npp):
    def mk(p):
        return pl.BlockSpec((None, None, rows, hd), lambda b, c, pt: (layer, pt[b, c * npp + p], 0, 0))
    return [mk(p) for p in range(npp)]


def _dec_scores(q, cache, layer, page_table, *, nh, hd):
    Bd, n_pages = page_table.shape
    npp = min(DEC_PAGES_PER_STEP, n_pages)
    rows = cache.shape[2]
    kern = functools.partial(_dec_scores_kernel, nh=nh, npp=npp)
    grid_spec = pltpu.PrefetchScalarGridSpec(
        num_scalar_prefetch=1, grid=(Bd, n_pages // npp),
        in_specs=[pl.BlockSpec((None, nh, hd), lambda b, c, pt: (b, 0, 0))] + _page_specs(layer, rows, hd, npp),
        out_specs=pl.BlockSpec((None, npp * nh, LANES), lambda b, c, pt: (b, c, 0)),
    )
    return pl.pallas_call(
        kern, grid_spec=grid_spec,
        out_shape=jax.ShapeDtypeStruct((Bd, n_pages * nh, LANES), F32),
        compiler_params=_cparams(("parallel", "parallel"), 48), name="dec_scores",
    )(page_table, q, *([cache] * npp))


def _dec_pv_kernel(pt_ref, w_ref, *refs, nh, npp, with_self):
    v_refs = refs[:npp]
    rest = refs[npp:]
    if with_self:
        wself_ref, vnew_ref, o_ref, acc_ref = rest
    else:
        o_ref, acc_ref = rest
    c = pl.program_id(1)

    @pl.when(c == 0)
    def _():
        acc_ref[...] = jnp.zeros_like(acc_ref)

    hmask = _head_mask(nh)
    acc = acc_ref[...]
    for p in range(npp):
        wd = w_ref[p * nh:(p + 1) * nh, :]
        wsel = jnp.concatenate(
            [jnp.where(hmask, jnp.broadcast_to(wd[r:r + 1, :], (nh, LANES)), 0.0) for r in range(nh)], axis=1)
        acc = acc + _dot(wsel.astype(BF16), v_refs[p][...].astype(BF16))
    acc_ref[...] = acc

    @pl.when(c == pl.num_programs(1) - 1)
    def _():
        if with_self:
            lane = lax.broadcasted_iota(jnp.int32, (nh, LANES), 1)
            wcol = jnp.sum(jnp.where(hmask & (lane < nh), jnp.broadcast_to(wself_ref[...], (nh, LANES)), 0.0),
                           axis=1, keepdims=True)
            o_ref[...] = acc + wcol * vnew_ref[...]
        else:
            o_ref[...] = acc


def _dec_pv(w, cache, layer, page_table, *, nh, hd, wself=None, vnew=None):
    Bd, n_pages = page_table.shape
    npp = min(DEC_PAGES_PER_STEP, n_pages)
    rows = cache.shape[2]
    with_self = wself is not None
    in_specs = ([pl.BlockSpec((None, npp * nh, LANES), lambda b, c, pt: (b, c, 0))]
                + _page_specs(layer, rows, hd, npp))
    args = [w] + [cache] * npp
    if with_self:
        in_specs += [pl.BlockSpec((None, 1, LANES), lambda b, c, pt: (b, 0, 0)),
                     pl.BlockSpec((None, nh, hd), lambda b, c, pt: (b, 0, 0))]
        args += [wself, vnew]
    kern = functools.partial(_dec_pv_kernel, nh=nh, npp=npp, with_self=with_self)
    grid_spec = pltpu.PrefetchScalarGridSpec(
        num_scalar_prefetch=1, grid=(Bd, n_pages // npp), in_specs=in_specs,
        out_specs=pl.BlockSpec((None, nh, hd), lambda b, c, pt: (b, 0, 0)),
        scratch_shapes=[pltpu.VMEM((nh, hd), F32)],
    )
    return pl.pallas_call(
        kern, grid_spec=grid_spec,
        out_shape=jax.ShapeDtypeStruct((Bd, nh, hd), F32),
        compiler_params=_cparams(("parallel", "arbitrary"), 48), name="dec_pv",
    )(page_table, *args)


def _suffix_in_row(x, nh):
    lane = lax.broadcasted_iota(jnp.int32, x.shape, 1)
    y = x
    s = nh
    while s < LANES:
        y = y + jnp.where(lane + s < LANES, pltpu.roll(y, LANES - s, axis=1), 0.0)
        s *= 2
    return y


def _allreduce_in_row(x, nh, op):
    s = nh
    while s < LANES:
        x = op(x, pltpu.roll(x, s, axis=1))
        s *= 2
    return x


DEC_WEIGHT_SEQS = 8


def _dec_sb_weights_kernel(z_ref, u_ref, w_ref, *, nh):
    u = u_ref[...]
    for j in range(DEC_WEIGHT_SEQS):
        z = z_ref[j]
        sp = _softplus(z)
        in_row = _suffix_in_row(sp, nh)
        row_tot = _allreduce_in_row(sp, nh, jnp.add)
        later_rows = _dot_left_exact(u, row_tot)
        later = in_row - sp + later_rows
        w_ref[j] = jnp.exp(z - sp - later)


def _dec_sb_weights(z, ustrict, *, nh):
    Bd, R, _ = z.shape
    nb = DEC_WEIGHT_SEQS
    blk = pl.BlockSpec((nb, R, LANES), lambda b: (b, 0, 0))
    kern = functools.partial(_dec_sb_weights_kernel, nh=nh)
    return pl.pallas_call(
        kern, grid=(Bd // nb,),
        in_specs=[blk, _full(ustrict.shape)], out_specs=blk,
        out_shape=jax.ShapeDtypeStruct(z.shape, F32),
        compiler_params=_cparams(("parallel",), 32), name="dec_sb_weights",
    )(z, ustrict)


def _dec_fox_weights_kernel(pt_ref, z_ref, q_ref, kn_ref, lfn_ref, u_ref, *refs, nh, n_pages, scale):
    lf_refs = refs[:n_pages]
    w_ref, wself_ref = refs[n_pages:]
    lf = jnp.concatenate([r[...] for r in lf_refs], axis=0)
    in_row = _suffix_in_row(lf, nh)
    row_tot = _allreduce_in_row(lf, nh, jnp.add)
    later_rows = _dot_left_exact(u_ref[...], row_tot)
    decay = in_row - lf + later_rows + lfn_ref[...]
    logits = z_ref[...] + decay
    hmask = _head_mask(nh)
    s_col = jnp.sum(q_ref[...] * kn_ref[...], axis=1, keepdims=True) * scale
    s_lane = jnp.sum(jnp.where(hmask, jnp.broadcast_to(s_col, (nh, LANES)), 0.0), axis=0, keepdims=True)
    m = _allreduce_in_row(jnp.max(logits, axis=0, keepdims=True), nh, jnp.maximum)
    m = jnp.maximum(m, s_lane)
    p = jnp.exp(logits - m)
    p_self = jnp.exp(s_lane - m)
    denom = _allreduce_in_row(jnp.sum(p, axis=0, keepdims=True), nh, jnp.add) + p_self
    w_ref[...] = p / denom
    wself_ref[...] = p_self / denom


def _dec_fox_weights(z, q, kn, lfn_lane, ustrict, logf_dense, layer, page_table, *, nh, hd):
    Bd, n_pages = page_table.shape
    R = z.shape[1]
    lf_specs = [pl.BlockSpec((None, None, nh, LANES), (lambda p: lambda b, pt: (layer, pt[b, p], 0, 0))(p))
                for p in range(n_pages)]
    kern = functools.partial(_dec_fox_weights_kernel, nh=nh, n_pages=n_pages, scale=float(hd ** -0.5))
    grid_spec = pltpu.PrefetchScalarGridSpec(
        num_scalar_prefetch=1, grid=(Bd,),
        in_specs=[pl.BlockSpec((None, R, LANES), lambda b, pt: (b, 0, 0)),
                  pl.BlockSpec((None, nh, hd), lambda b, pt: (b, 0, 0)),
                  pl.BlockSpec((None, nh, hd), lambda b, pt: (b, 0, 0)),
                  pl.BlockSpec((None, 1, LANES), lambda b, pt: (b, 0, 0)),
                  pl.BlockSpec(ustrict.shape, lambda b, pt: (0, 0))] + lf_specs,
        out_specs=(pl.BlockSpec((None, R, LANES), lambda b, pt: (b, 0, 0)),
                   pl.BlockSpec((None, 1, LANES), lambda b, pt: (b, 0, 0))),
    )
    return pl.pallas_call(
        kern, grid_spec=grid_spec,
        out_shape=(jax.ShapeDtypeStruct(z.shape, F32), jax.ShapeDtypeStruct((Bd, 1, LANES), F32)),
        compiler_params=_cparams(("parallel",), 32), name="dec_fox_weights",
    )(page_table, z, q, kn, lfn_lane, ustrict, *([logf_dense] * n_pages))


def _lower_incl(n):
    i = jnp.arange(n)
    return (i[None, :] <= i[:, None]).astype(BF16)


def _upper_strict(n):
    i = jnp.arange(n)
    return (i[None, :] > i[:, None]).astype(BF16)


def _block_diag(n, blk):
    i = jnp.arange(n)
    return ((i[:, None] // blk) == (i[None, :] // blk)).astype(BF16)


def _bias_placement(nh, hd):
    e = jnp.zeros((3, nh, nh * LANES), F32)
    h = jnp.arange(nh)
    for p in range(3):
        e = e.at[p, h, h * LANES + hd + p].set(1.0)
    return e.astype(BF16)


def kernel(x_prompt, x_sample, state_hgrn, cache_sb_k, cache_sb_v, cache_fox_k, cache_fox_v, cache_fox_logf,
           page_table, norm_g, w_in_ab, w_out_ab, lb_param, out_norm_a, w_in_c, b_f_c, w_out_c,
           qk_norm_q, qk_norm_k):
    B, T, D = x_prompt.shape
    Bd = x_sample.shape[0]
    n_ab, _, H_A, DK_A, DV_A = state_hgrn.shape
    n_c = cache_fox_k.shape[0]
    pool, page_size, H_B, HD_B = cache_sb_k.shape[1:]
    H_C, HD_C = cache_fox_k.shape[3:]
    FA, WA, WB, WC = H_A * DK_A, H_A * DV_A, H_B * HD_B, H_C * HD_C
    depth = norm_g.shape[0]
    assert DK_A == DV_A == LANES and HD_B == HD_C == LANES // 2 and page_size == LANES
    assert w_in_ab.shape[2] == 2 * FA + 2 * WA + 4 * WB and w_in_c.shape[2] == 4 * WC + H_C

    tm = min(256, T)
    tc = min(512, T)
    assert T % tm == 0 and T % tc == 0 and tc % HGRN_CHUNK == 0 and Bd % HGRN_STEP_SEQS == 0
    nt = T // tm

    p = jax.nn.softmax(lb_param.astype(F32), axis=0)
    lower_bounds = jnp.cumsum(p, axis=0) - p[0]

    ltri_chunk = _lower_incl(HGRN_CHUNK)
    ltri_tm = _lower_incl(tm)
    ustrict_tm = _upper_strict(tm)
    bd = _block_diag(256, HD_C)
    emat = _bias_placement(H_C, HD_C)
    n_pages = page_table.shape[1]
    ustrict_sb = _upper_strict(n_pages * H_B)
    ustrict_fox = _upper_strict(n_pages * H_C)

    sbk = cache_sb_k.reshape(n_ab, pool, page_size * H_B, HD_B)
    sbv = cache_sb_v.reshape(n_ab, pool, page_size * H_B, HD_B)
    fxk = cache_fox_k.reshape(n_c, pool, page_size * H_C, HD_C)
    fxv = cache_fox_v.reshape(n_c, pool, page_size * H_C, HD_C)
    fxlf = cache_fox_logf.reshape(n_c, pool, H_C, LANES)

    xs = x_sample.reshape(1, Bd, D)
    yp, ys = x_prompt, xs
    hgrn_p, hgrn_s, sbk_p, sbv_p, sbk_s, sbv_s = [], [], [], [], [], []
    fk_p, fv_p, fl_p, fk_s, fv_s, fl_s = [], [], [], [], [], []

    for l in range(depth):
        j = l // 2
        g_pre = norm_g[l].reshape(1, D)
        if l % 2 == 0:
            w_in = w_in_ab[j].astype(BF16)
            w_out = w_out_ab[j].astype(BF16)
            lb = lower_bounds[j].reshape(1, FA)
            g_out = out_norm_a[j].reshape(1, DV_A)
            qa, ka, lf, va, ga, gb, kbo, vbo, qb, kb, vbt = _inproj_ab(yp, g_pre, w_in, lb, fa=FA, wa=WA, wb=WB,
                                                                       hd_b=HD_B, tm=tm)
            oa, S = _hgrn_prompt(qa, ka, va, lf, ltri_chunk, nh=H_A, dk=DK_A, tc=tc)
            obt = _sb_prompt(qb, kb, vbt.reshape(B, nt, H_B, HD_B, tm), ustrict_tm, nh=H_B, hd=HD_B, tq=tm)
            yp = _outproj_ab(yp, oa, ga, obt, gb, g_out, w_out, nh=H_A, dv=DV_A, tm=tm, ob_transposed=True)
            hgrn_p.append(S)
            sbk_p.append(kbo.reshape(B, T, H_B, HD_B))
            sbv_p.append(vbo.reshape(B, T, H_B, HD_B))
            qa, ka, lf, va, ga, gb, kbo, vbo, qb, _, _ = _inproj_ab(ys, g_pre, w_in, lb, fa=FA, wa=WA, wb=WB,
                                                                    hd_b=HD_B, tm=Bd)
            oa, S = _hgrn_step(qa[0], ka[0], va[0], lf[0], state_hgrn[j], nh=H_A, dk=DK_A)
            qd = qb[0].astype(F32).reshape(Bd, H_B, HD_B)
            z = _dec_scores(qd, sbk, j, page_table, nh=H_B, hd=HD_B)
            w = _dec_sb_weights(z, ustrict_sb, nh=H_B)
            ob = _dec_pv(w, sbv, j, page_table, nh=H_B, hd=HD_B)
            ys = _outproj_ab(ys, oa[None], ga, ob.reshape(1, Bd, WB), gb, g_out, w_out, nh=H_A, dv=DV_A, tm=Bd,
                             ob_transposed=False)
            hgrn_s.append(S)
            sbk_s.append(kbo.reshape(Bd, 1, H_B, HD_B))
            sbv_s.append(vbo.reshape(Bd, 1, H_B, HD_B))
        else:
            w_main = w_in_c[j][:, :4 * WC].astype(BF16)
            w_f = jnp.pad(w_in_c[j][:, 4 * WC:], ((0, 0), (0, LANES - H_C))).astype(BF16)
            w_out = w_out_c[j].astype(BF16)
            b_f = b_f_c[j].reshape(1, H_C)
            gq = jnp.tile(qk_norm_q[j], H_C).reshape(1, WC)
            gk = jnp.tile(qk_norm_k[j], H_C).reshape(1, WC)
            qp, kn, v, gate, logf, vt = _inproj_c(yp, g_pre, w_main, w_f, b_f, gq, gk, bd, wc=WC, hd=HD_C, nh=H_C,
                                                  tm=tm, decode=False)
            kp = _fox_prep(kn, logf, ltri_tm, emat, nh=H_C, hd=HD_C, tm=tm)
            ot = _fox_prompt(qp, kp, vt.reshape(B, nt, H_C, HD_C, tm), nh=H_C, hd=HD_C, tq=tm)
            yp = _outproj_c(yp, ot, gate, w_out, tm=tm, o_transposed=True)
            fk_p.append(kn.reshape(B, T, H_C, HD_C))
            fv_p.append(v.reshape(B, T, H_C, HD_C))
            fl_p.append(logf)
            qn, kn, v, gate, logf = _inproj_c(ys, g_pre, w_main, w_f, b_f, gq, gk, bd, wc=WC, hd=HD_C, nh=H_C,
                                              tm=Bd, decode=True)
            qd = qn[0].reshape(Bd, H_C, HD_C)
            kd = kn[0].reshape(Bd, H_C, HD_C)
            vd = v[0].reshape(Bd, H_C, HD_C)
            lfn_lane = jnp.tile(logf[0], (1, LANES // H_C)).reshape(Bd, 1, LANES)
            z = _dec_scores(qd * (HD_C ** -0.5), fxk, j, page_table, nh=H_C, hd=HD_C)
            w, wself = _dec_fox_weights(z, qd, kd, lfn_lane, ustrict_fox, fxlf, j, page_table, nh=H_C, hd=HD_C)
            o = _dec_pv(w, fxv, j, page_table, nh=H_C, hd=HD_C, wself=wself, vnew=vd)
            ys = _outproj_c(ys, o.reshape(1, Bd, WC), gate, w_out, tm=Bd, o_transposed=False)
            fk_s.append(kd.reshape(Bd, 1, H_C, HD_C))
            fv_s.append(vd.reshape(Bd, 1, H_C, HD_C))
            fl_s.append(logf[0].reshape(Bd, 1, H_C))

    return (yp, ys.reshape(Bd, 1, D), jnp.stack(hgrn_p), jnp.stack(hgrn_s), jnp.stack(sbk_p), jnp.stack(sbv_p),
            jnp.stack(sbk_s), jnp.stack(sbv_s), jnp.stack(fk_p), jnp.stack(fv_p), jnp.stack(fl_p),
            jnp.stack(fk_s), jnp.stack(fv_s), jnp.stack(fl_s))
```

```python
import functools

import jax
import jax.numpy as jnp
from jax import lax
from jax.experimental import pallas as pl
from jax.experimental.pallas import tpu as pltpu

F32 = jnp.float32
BF16 = jnp.bfloat16
EPS = 1e-6
NEG_BIG = -1e30
LANES = 128
MIB = 1024 * 1024
NT_DIMS = (((1,), (1,)), ((), ()))


def _cparams(semantics, vmem_mib):
    return pltpu.CompilerParams(dimension_semantics=semantics, vmem_limit_bytes=vmem_mib * MIB)


def _dot(a, b):
    return jnp.dot(a, b, preferred_element_type=F32)


def _dot_nt(a, b):
    return lax.dot_general(a, b, NT_DIMS, preferred_element_type=F32)


def _split2(x):
    hi = x.astype(BF16)
    lo = (x - hi.astype(F32)).astype(BF16)
    return hi, lo


def _split3(x):
    hi = x.astype(BF16)
    r = x - hi.astype(F32)
    mid = r.astype(BF16)
    lo = (r - mid.astype(F32)).astype(BF16)
    return hi, mid, lo


def _dot_left_exact(mat01, x):
    hi, mid, lo = _split3(x)
    return _dot(mat01, hi) + _dot(mat01, mid) + _dot(mat01, lo)


def _sigmoid(x):
    return 1.0 / (1.0 + jnp.exp(-x))


def _softplus(x):
    return jnp.maximum(x, 0.0) + jnp.log(1.0 + jnp.exp(-jnp.abs(x)))


def _rms(x, g):
    return x * lax.rsqrt(jnp.mean(x * x, axis=-1, keepdims=True) + EPS) * g


def _full(shape):
    nd = len(shape)
    return pl.BlockSpec(shape, lambda *_: (0,) * nd)


def _inproj_ab_kernel(x_ref, g_ref, w_ref, lb_ref,
                      qa_ref, ka_ref, lf_ref, va_ref, ga_ref, gb_ref, kbo_ref, vbo_ref,
                      qb_ref, kb_ref, vbt_ref, *, fa, wa, wb, sb_scale):
    xn = _rms(x_ref[...], g_ref[...]).astype(BF16)
    offs = [0]

    def proj(n):
        lo = offs[0]
        offs[0] = lo + n
        return _dot(xn, w_ref[:, lo:lo + n])

    qa = proj(fa)
    za = proj(fa)
    ia = proj(wa)
    ga = proj(wa)
    qb = proj(wb)
    kb = proj(wb)
    vb = proj(wb)
    gb = proj(wb)
    lb = lb_ref[...]
    lf_ref[...] = jnp.log(lb + (1.0 - lb) * _sigmoid(za))
    ka_ref[...] = (1.0 - lb) * _sigmoid(-za)
    qa_ref[...] = qa * _sigmoid(qa)
    va_ref[...] = ia
    ga_ref[...] = ga * _sigmoid(ga)
    gb_ref[...] = gb * _sigmoid(gb)
    kbo_ref[...] = kb
    vbo_ref[...] = vb
    qb_ref[...] = (qb * sb_scale).astype(BF16)
    kb_ref[...] = kb.astype(BF16)
    vbt_ref[...] = vb.T.astype(BF16)


def _inproj_ab(x, g, w_bf16, lb, *, fa, wa, wb, hd_b, tm):
    B, T, D = x.shape
    nt = T // tm
    cols = w_bf16.shape[1]
    row = lambda n: pl.BlockSpec((None, tm, n), lambda b, t: (b, t, 0))
    f32o = lambda n: jax.ShapeDtypeStruct((B, T, n), F32)
    out_shape = (f32o(fa), f32o(fa), f32o(fa), f32o(wa), f32o(wa), f32o(wb), f32o(wb), f32o(wb),
                 jax.ShapeDtypeStruct((B, T, wb), BF16), jax.ShapeDtypeStruct((B, T, wb), BF16),
                 jax.ShapeDtypeStruct((B, nt, wb, tm), BF16))
    out_specs = (row(fa), row(fa), row(fa), row(wa), row(wa), row(wb), row(wb), row(wb), row(wb), row(wb),
                 pl.BlockSpec((None, None, wb, tm), lambda b, t: (b, t, 0, 0)))
    kern = functools.partial(_inproj_ab_kernel, fa=fa, wa=wa, wb=wb, sb_scale=float(hd_b ** -0.5))
    return pl.pallas_call(
        kern, grid=(B, nt),
        in_specs=[row(D), _full((1, D)), _full((D, cols)), _full((1, fa))],
        out_specs=out_specs, out_shape=out_shape,
        compiler_params=_cparams(("parallel", "parallel"), 48), name="inproj_ab",
    )(x, g, w_bf16, lb)


def _outproj_ab_kernel(x_ref, oa_ref, ga_ref, ob_ref, gb_ref, gout_ref, w_ref, y_ref, *, nh, dv, ob_transposed):
    oa = oa_ref[...]
    gout = gout_ref[...]
    parts = [_rms(oa[:, h * dv:(h + 1) * dv], gout) for h in range(nh)]
    oa_n = jnp.concatenate(parts, axis=1) * ga_ref[...]
    ob = ob_ref[...]
    if ob_transposed:
        ob = ob.T
    ob = ob * gb_ref[...]
    cat = jnp.concatenate([oa_n, ob], axis=1).astype(BF16)
    y_ref[...] = x_ref[...] + _dot(cat, w_ref[...])


def _outproj_ab(x, oa, ga, ob, gb, gout, w_bf16, *, nh, dv, tm, ob_transposed):
    B, T, D = x.shape
    nt = T // tm
    wa = oa.shape[-1]
    wb = gb.shape[-1]
    row = lambda n: pl.BlockSpec((None, tm, n), lambda b, t: (b, t, 0))
    if ob_transposed:
        ob_spec = pl.BlockSpec((None, wb, tm), lambda b, t: (b, 0, t))
    else:
        ob_spec = row(wb)
    kern = functools.partial(_outproj_ab_kernel, nh=nh, dv=dv, ob_transposed=ob_transposed)
    return pl.pallas_call(
        kern, grid=(B, nt),
        in_specs=[row(D), row(wa), row(wa), ob_spec, row(wb), _full((1, dv)), _full((wa + wb, D))],
        out_specs=row(D), out_shape=jax.ShapeDtypeStruct((B, T, D), F32),
        compiler_params=_cparams(("parallel", "parallel"), 40), name="outproj_ab",
    )(x, oa, ga, ob, gb, gout, w_bf16)


def _inproj_c_kernel(x_ref, g_ref, w_ref, wf_ref, bf_ref, gq_ref, gk_ref, bd_ref, *out_refs,
                     wc, hd, nh, nb, fox_scale, decode):
    xn = _rms(x_ref[...], g_ref[...]).astype(BF16)
    bd = bd_ref[...]
    inv_hd = 1.0 / hd

    def normed(base, gfull_ref):
        outs = []
        for c in range(wc // nb):
            y = _dot(xn, w_ref[:, base + c * nb: base + (c + 1) * nb])
            hi, lo = _split2(y * y)
            ms = (_dot(hi, bd) + _dot(lo, bd)) * inv_hd
            outs.append(y * lax.rsqrt(ms + EPS) * gfull_ref[:, c * nb:(c + 1) * nb])
        return jnp.concatenate(outs, axis=1)

    qn = normed(0, gq_ref)
    kn = normed(wc, gk_ref)
    v = _dot(xn, w_ref[:, 2 * wc:3 * wc])
    gt = _dot(xn, w_ref[:, 3 * wc:4 * wc])
    f = _dot(xn, wf_ref[...])[:, :nh] + bf_ref[...]
    logf = -_softplus(-f)
    if decode:
        qn_ref, k_ref, v_ref, gate_ref, lf_ref = out_refs
        qn_ref[...] = qn
    else:
        qp_ref, k_ref, v_ref, gate_ref, lf_ref, vt_ref = out_refs
        lane = lax.broadcasted_iota(jnp.int32, (qn.shape[0], LANES), 1)
        ones_cols = jnp.where((lane >= hd) & (lane < hd + 3), 1.0, 0.0)
        for h in range(nh):
            pair = qn[:, (h // 2) * LANES:(h // 2 + 1) * LANES]
            if h % 2 == 1:
                pair = pltpu.roll(pair, hd, axis=1)
            qp_ref[h] = jnp.where(lane < hd, pair * fox_scale, ones_cols).astype(BF16)
        vt_ref[...] = v.T.astype(BF16)
    k_ref[...] = kn
    v_ref[...] = v
    gate_ref[...] = gt * _sigmoid(gt)
    lf_ref[...] = logf


def _inproj_c(x, g, w_bf16, wf_bf16, bf, gq_full, gk_full, bd, *, wc, hd, nh, tm, decode):
    B, T, D = x.shape
    nt = T // tm
    nb = bd.shape[0]
    row = lambda n: pl.BlockSpec((None, tm, n), lambda b, t: (b, t, 0))
    f32o = lambda n: jax.ShapeDtypeStruct((B, T, n), F32)
    common_shapes = (f32o(wc), f32o(wc), f32o(wc), f32o(nh))
    common_specs = (row(wc), row(wc), row(wc), row(nh))
    if decode:
        out_shape = (f32o(wc),) + common_shapes
        out_specs = (row(wc),) + common_specs
    else:
        out_shape = ((jax.ShapeDtypeStruct((B, nh, T, LANES), BF16),) + common_shapes
                     + (jax.ShapeDtypeStruct((B, nt, wc, tm), BF16),))
        out_specs = ((pl.BlockSpec((None, nh, tm, LANES), lambda b, t: (b, 0, t, 0)),) + common_specs
                     + (pl.BlockSpec((None, None, wc, tm), lambda b, t: (b, t, 0, 0)),))
    kern = functools.partial(_inproj_c_kernel, wc=wc, hd=hd, nh=nh, nb=nb, fox_scale=float(hd ** -0.5 * LOG2E), decode=decode)
    return pl.pallas_call(
        kern, grid=(B, nt),
        in_specs=[row(D), _full((1, D)), _full((D, 4 * wc)), _full((D, LANES)), _full((1, nh)),
                  _full((1, wc)), _full((1, wc)), _full((nb, nb))],
        out_specs=out_specs, out_shape=out_shape,
        compiler_params=_cparams(("parallel", "parallel"), 48), name="inproj_c",
    )(x, g, w_bf16, wf_bf16, bf, gq_full, gk_full, bd)


def _outproj_c_kernel(x_ref, o_ref, gate_ref, w_ref, y_ref, *, o_transposed):
    o = o_ref[...]
    if o_transposed:
        o = o.T
    y_ref[...] = x_ref[...] + _dot((o * gate_ref[...]).astype(BF16), w_ref[...])


def _outproj_c(x, o, gate, w_bf16, *, tm, o_transposed):
    B, T, D = x.shape
    nt = T // tm
    wc = gate.shape[-1]
    row = lambda n: pl.BlockSpec((None, tm, n), lambda b, t: (b, t, 0))
    o_spec = pl.BlockSpec((None, wc, tm), lambda b, t: (b, 0, t)) if o_transposed else row(wc)
    kern = functools.partial(_outproj_c_kernel, o_transposed=o_transposed)
    return pl.pallas_call(
        kern, grid=(B, nt),
        in_specs=[row(D), o_spec, row(wc), _full((wc, D))],
        out_specs=row(D), out_shape=jax.ShapeDtypeStruct((B, T, D), F32),
        compiler_params=_cparams(("parallel", "parallel"), 40), name="outproj_c",
    )(x, o, gate, w_bf16)


HGRN_CHUNK = 128
HGRN_SUB = 16
HGRN_EXP_CLAMP = 60.0


def _hgrn_prompt_kernel(q_ref, k_ref, v_ref, g_ref, ltri_ref, o_ref, s_ref, st_ref, *, nh, dk, tc):
    t = pl.program_id(1)

    @pl.when(t == 0)
    def _():
        st_ref[...] = jnp.zeros_like(st_ref)

    C, SB = HGRN_CHUNK, HGRN_SUB
    ltri = ltri_ref[...]

    def chunk(c, carry):
        r0 = pl.multiple_of(c * C, C)
        for h in range(nh):
            sl = slice(h * dk, (h + 1) * dk)
            q = q_ref[pl.ds(r0, C), sl]
            k = k_ref[pl.ds(r0, C), sl]
            v = v_ref[pl.ds(r0, C), sl]
            g = g_ref[pl.ds(r0, C), sl]
            G = _dot_left_exact(ltri, g)
            st = st_ref[h]
            o = _dot_nt((q * jnp.exp(G)).astype(BF16), st.astype(BF16))
            vb = v.astype(BF16)
            outs = []
            for i in range(C // SB):
                a, n = i * SB, (i + 1) * SB
                r = G[a - 1:a, :] if i > 0 else jnp.zeros((1, dk), F32)
                qt = (q[a:n] * jnp.exp(G[a:n] - r)).astype(BF16)
                kt = (k[:n] * jnp.exp(jnp.minimum(r - G[:n], HGRN_EXP_CLAMP))).astype(BF16)
                sc = _dot_nt(qt, kt)
                row = lax.broadcasted_iota(jnp.int32, (SB, n), 0) + a
                col = lax.broadcasted_iota(jnp.int32, (SB, n), 1)
                sc = jnp.where(col <= row, sc, 0.0)
                outs.append(_dot(sc.astype(BF16), vb[:n]))
            o_ref[pl.ds(r0, C), sl] = o + jnp.concatenate(outs, axis=0)
            gend = G[C - 1:C, :]
            kbar = (k * jnp.exp(gend - G)).astype(BF16)
            st_ref[h] = st * jnp.exp(gend) + _dot(v.T.astype(BF16), kbar)
        return carry

    lax.fori_loop(0, tc // C, chunk, 0)

    @pl.when(t == pl.num_programs(1) - 1)
    def _():
        for h in range(nh):
            s_ref[h] = st_ref[h].T


def _hgrn_prompt(q, k, v, g, ltri, *, nh, dk, tc):
    B, T, W = q.shape
    nt = T // tc
    row = pl.BlockSpec((None, tc, W), lambda b, t: (b, t, 0))
    kern = functools.partial(_hgrn_prompt_kernel, nh=nh, dk=dk, tc=tc)
    return pl.pallas_call(
        kern, grid=(B, nt),
        in_specs=[row, row, row, row, _full(ltri.shape)],
        out_specs=(row, pl.BlockSpec((None, nh, dk, dk), lambda b, t: (b, 0, 0, 0))),
        out_shape=(jax.ShapeDtypeStruct((B, T, W), F32), jax.ShapeDtypeStruct((B, nh, dk, dk), F32)),
        scratch_shapes=[pltpu.VMEM((nh, dk, dk), F32)],
        compiler_params=_cparams(("parallel", "arbitrary"), 32), name="hgrn_prompt",
    )(q, k, v, g, ltri)


HGRN_STEP_SEQS = 8


def _hgrn_step_kernel(q_ref, k_ref, v_ref, g_ref, s_ref, o_ref, so_ref, *, nh, dk):
    nb = HGRN_STEP_SEQS
    pad = jnp.zeros((dk - nb, dk), F32)

    def columns(x8):
        return jnp.concatenate([x8, pad], axis=0).T

    for h in range(nh):
        sl = slice(h * dk, (h + 1) * dk)
        egc = columns(jnp.exp(g_ref[:, sl]))
        kc = columns(k_ref[:, sl])
        qc = columns(q_ref[:, sl])
        for j in range(nb):
            s_new = egc[:, j:j + 1] * s_ref[j, h] + kc[:, j:j + 1] * v_ref[j:j + 1, sl]
            so_ref[j, h] = s_new
            o_ref[j:j + 1, sl] = jnp.sum(qc[:, j:j + 1] * s_new, axis=0, keepdims=True)


def _hgrn_step(q, k, v, g, s0, *, nh, dk):
    Bd, W = q.shape
    nb = HGRN_STEP_SEQS
    row = pl.BlockSpec((nb, W), lambda b: (b, 0))
    st = pl.BlockSpec((nb, nh, dk, dk), lambda b: (b, 0, 0, 0))
    kern = functools.partial(_hgrn_step_kernel, nh=nh, dk=dk)
    return pl.pallas_call(
        kern, grid=(Bd // nb,),
        in_specs=[row, row, row, row, st],
        out_specs=(row, st),
        out_shape=(jax.ShapeDtypeStruct((Bd, W), F32), jax.ShapeDtypeStruct(s0.shape, F32)),
        compiler_params=_cparams(("parallel",), 32), name="hgrn_step",
    )(q, k, v, g, s0)


SB_SKIP = 105.0
FOX_SKIP = 152.0
FOX_TQ = 512
FOX_HEADS = 2
LOG2E = 1.4426950408889634
FOX_BOUND_SLACK = 1.01


def _sb_prompt_kernel(q_ref, k_ref, vt_ref, u_ref, o_ref, *, tq, hd):
    i = pl.program_id(2)
    lane = lax.broadcasted_iota(jnp.int32, (tq, LANES), 1)
    q2 = q_ref[...]
    zero = jnp.zeros_like(q2)
    qs = (jnp.where(lane < hd, q2, zero), jnp.where(lane >= hd, q2, zero))
    u = u_ref[...]
    srow = lax.broadcasted_iota(jnp.int32, (tq, tq), 0)
    tcol = lax.broadcasted_iota(jnp.int32, (tq, tq), 1)
    causal = srow < tcol

    def block(j, state, masked):
        kb = k_ref[pl.ds(pl.multiple_of(j * tq, tq), tq), :]
        out = []
        for g in range(2):
            carry, acc = state[g]
            z = _dot_nt(kb, qs[g])
            sp = _softplus(z)
            if masked:
                sp = jnp.where(causal, sp, 0.0)
            hi, lo = _split2(sp)
            later = _dot(u, hi) + _dot(u, lo)
            w = jnp.exp(z - sp - later - carry)
            if masked:
                w = jnp.where(causal, w, 0.0)
            acc = acc + _dot(vt_ref[j, g], w.astype(BF16))
            carry = carry + jnp.sum(sp, axis=0, keepdims=True)
            out.append((carry, acc))
        return tuple(out)

    init = (jnp.zeros((1, tq), F32), jnp.zeros((hd, tq), F32))
    state = block(i, (init, init), True)

    def cond(st):
        j, state = st
        smallest = jnp.minimum(jnp.min(state[0][0]), jnp.min(state[1][0]))
        return (j >= 0) & (smallest <= SB_SKIP)

    def body(st):
        j, state = st
        return j - 1, block(j, state, False)

    _, state = lax.while_loop(cond, body, (i - 1, state))
    o_ref[...] = jnp.concatenate([state[0][1], state[1][1]], axis=0)


def _sb_prompt(qb, kb, vbt, ustrict, *, nh, hd, tq):
    B, T, W = qb.shape
    nt = T // tq
    kern = functools.partial(_sb_prompt_kernel, tq=tq, hd=hd)
    return pl.pallas_call(
        kern, grid=(B, nh // 2, nt),
        in_specs=[pl.BlockSpec((None, tq, LANES), lambda b, h, i: (b, i, h)),
                  pl.BlockSpec((None, T, LANES), lambda b, h, i: (b, 0, h)),
                  pl.BlockSpec((None, nt, 2, hd, tq), lambda b, h, i: (b, 0, h, 0, 0)),
                  _full((tq, tq))],
        out_specs=pl.BlockSpec((None, 2 * hd, tq), lambda b, h, i: (b, h, i)),
        out_shape=jax.ShapeDtypeStruct((B, W, T), F32),
        compiler_params=_cparams(("parallel", "parallel", "arbitrary"), 40), name="sb_prompt",
    )(qb, kb, vbt, ustrict)


def _fox_prep_kernel(kn_ref, lf_ref, ltri_ref, e_ref, kp_ref, clast_ref, carry_ref, *, nh, hd):
    t = pl.program_id(1)

    @pl.when(t == 0)
    def _():
        carry_ref[...] = jnp.zeros_like(carry_ref)

    c = carry_ref[...] + _dot_left_exact(ltri_ref[...], lf_ref[...])
    tm = c.shape[0]
    carry_ref[...] = c[tm - 1:tm, :]
    c2 = c * LOG2E
    clast_ref[...] = c2[tm - 1:tm, :]
    hi, mid, lo = _split3(-c2)
    aug = _dot(hi, e_ref[0]) + _dot(mid, e_ref[1]) + _dot(lo, e_ref[2])
    kn = kn_ref[...]
    lane = lax.broadcasted_iota(jnp.int32, (tm, LANES), 1)
    for h in range(nh):
        pair = kn[:, (h // 2) * LANES:(h // 2 + 1) * LANES]
        if h % 2 == 1:
            pair = pltpu.roll(pair, hd, axis=1)
        kp_ref[h] = jnp.where(lane < hd, pair, aug[:, h * LANES:(h + 1) * LANES]).astype(BF16)


def _fox_prep(kn, logf, ltri, emat, *, nh, hd, tm):
    B, T, W = kn.shape
    nt = T // tm
    kern = functools.partial(_fox_prep_kernel, nh=nh, hd=hd)
    return pl.pallas_call(
        kern, grid=(B, nt),
        in_specs=[pl.BlockSpec((None, tm, W), lambda b, t: (b, t, 0)),
                  pl.BlockSpec((None, tm, nh), lambda b, t: (b, t, 0)),
                  _full(ltri.shape), _full(emat.shape)],
        out_specs=(pl.BlockSpec((None, nh, tm, LANES), lambda b, t: (b, 0, t, 0)),
                   pl.BlockSpec((None, None, 1, nh), lambda b, t: (b, t, 0, 0))),
        out_shape=(jax.ShapeDtypeStruct((B, nh, T, LANES), BF16), jax.ShapeDtypeStruct((B, nt, 1, nh), F32)),
        scratch_shapes=[pltpu.VMEM((1, nh), F32)],
        compiler_params=_cparams(("parallel", "arbitrary"), 40), name="fox_prep",
    )(kn, logf, ltri, emat)


def _fox_prompt_kernel(bound_ref, cend_ref, q_ref, k_ref, vt_ref, o_ref, *, tq, tm, hd, G):
    b = pl.program_id(0)
    hg = pl.program_id(1)
    i = pl.program_id(2)
    nsub = tq // tm
    srow = lax.broadcasted_iota(jnp.int32, (tq, tq), 0)
    tcol = lax.broadcasted_iota(jnp.int32, (tq, tq), 1)
    causal = srow <= tcol

    def block(j, state, masked):
        out = []
        for g in range(G):
            m, l, acc = state[g]
            s = _dot_nt(k_ref[g, j], q_ref[g])
            if masked:
                s = jnp.where(causal, s, NEG_BIG)
            m_new = jnp.maximum(m, jnp.max(s, axis=0, keepdims=True))
            alpha = jnp.exp2(m - m_new)
            p = jnp.exp2(s - m_new)
            l = alpha * l + jnp.sum(p, axis=0, keepdims=True)
            pb = p.astype(BF16)
            pv = _dot(vt_ref[j * nsub, g], pb[:tm])
            for r in range(1, nsub):
                pv = pv + _dot(vt_ref[j * nsub + r, g], pb[r * tm:(r + 1) * tm])
            out.append((m_new, l, alpha * acc + pv))
        return tuple(out)

    init = (jnp.full((1, tq), NEG_BIG, F32), jnp.zeros((1, tq), F32), jnp.zeros((hd, tq), F32))
    state = block(i, (init,) * G, True)
    bound = bound_ref[0]

    def cond(st):
        j, state = st
        jj = jnp.maximum(j, 0)
        alive = None
        for g in range(G):
            ub = bound - cend_ref[b, hg * G + g, jj]
            a = ub - jnp.min(state[g][0]) > -FOX_SKIP
            alive = a if alive is None else (alive | a)
        return (j >= 0) & alive

    def body(st):
        j, state = st
        return j - 1, block(j, state, False)

    _, state = lax.while_loop(cond, body, (i - 1, state))
    o_ref[...] = jnp.concatenate([acc / l for (_, l, acc) in state], axis=0)


def _fox_prompt(qp, kp, vt, bound, cend, *, nh, hd, tq, tm):
    B, _, T, _ = qp.shape
    nt = T // tq
    G = FOX_HEADS
    kp5 = kp.reshape(B, nh, nt, tq, LANES)
    kern = functools.partial(_fox_prompt_kernel, tq=tq, tm=tm, hd=hd, G=G)
    grid_spec = pltpu.PrefetchScalarGridSpec(
        num_scalar_prefetch=2, grid=(B, nh // G, nt),
        in_specs=[pl.BlockSpec((None, G, tq, LANES), lambda b, h, i, *_: (b, h, i, 0)),
                  pl.BlockSpec((None, G, nt, tq, LANES), lambda b, h, i, *_: (b, h, 0, 0, 0)),
                  pl.BlockSpec((None, T // tm, G, hd, tm), lambda b, h, i, *_: (b, 0, h, 0, 0))],
        out_specs=pl.BlockSpec((None, G * hd, tq), lambda b, h, i, *_: (b, h, i)),
    )
    return pl.pallas_call(
        kern, grid_spec=grid_spec,
        out_shape=jax.ShapeDtypeStruct((B, nh * hd, T), F32),
        compiler_params=_cparams(("parallel", "parallel", "arbitrary"), 48), name="fox_prompt",
    )(bound, cend, qp, kp5, vt)


DEC_PAGES_PER_STEP = 8


def _head_mask(nh):
    sub = lax.broadcasted_iota(jnp.int32, (nh, LANES), 0)
    lane = lax.broadcasted_iota(jnp.int32, (nh, LANES), 1)
    return (lane % nh) == sub


def _dec_scores_kernel(pt_ref, q_ref, *refs, nh, npp):
    k_refs, z_ref = refs[:npp], refs[npp]
    qb = q_ref[...].astype(BF16)
    hmask = _head_mask(nh)
    rowid = lax.broadcasted_iota(jnp.int32, (nh, LANES), 0)
    for p in range(npp):
        kf = k_refs[p][...].reshape(-1, qb.shape[1]).astype(BF16)
        zall = _dot_nt(qb, kf)
        zd = jnp.zeros((nh, LANES), F32)
        for r in range(nh):
            picked = jnp.sum(jnp.where(hmask, zall[:, r * LANES:(r + 1) * LANES], 0.0), axis=0, keepdims=True)
            zd = jnp.where(rowid == r, picked, zd)
        z_ref[p * nh:(p + 1) * nh, :] = zd


def _page_specs(layer, cache, npp):
    blk = (None, None) + tuple(cache.shape[2:])

    def mk(p):
        return pl.BlockSpec(blk, lambda b, c, pt: (layer, pt[b, c * npp + p], 0, 0, 0))
    return [mk(p) for p in range(npp)]


def _dec_scores(q, cache, layer, page_table, *, nh, hd):
    Bd, n_pages = page_table.shape
    npp = min(DEC_PAGES_PER_STEP, n_pages)
    kern = functools.partial(_dec_scores_kernel, nh=nh, npp=npp)
    grid_spec = pltpu.PrefetchScalarGridSpec(
        num_scalar_prefetch=1, grid=(Bd, n_pages // npp),
        in_specs=[pl.BlockSpec((None, nh, hd), lambda b, c, pt: (b, 0, 0))] + _page_specs(layer, cache, npp),
        out_specs=pl.BlockSpec((None, npp * nh, LANES), lambda b, c, pt: (b, c, 0)),
    )
    return pl.pallas_call(
        kern, grid_spec=grid_spec,
        out_shape=jax.ShapeDtypeStruct((Bd, n_pages * nh, LANES), F32),
        compiler_params=_cparams(("parallel", "parallel"), 48), name="dec_scores",
    )(page_table, q, *([cache] * npp))


def _dec_pv_kernel(pt_ref, w_ref, *refs, nh, npp, with_self):
    v_refs = refs[:npp]
    rest = refs[npp:]
    if with_self:
        wself_ref, vnew_ref, o_ref, acc_ref = rest
    else:
        o_ref, acc_ref = rest
    c = pl.program_id(1)

    @pl.when(c == 0)
    def _():
        acc_ref[...] = jnp.zeros_like(acc_ref)

    hmask = _head_mask(nh)
    acc = acc_ref[...]
    for p in range(npp):
        wd = w_ref[p * nh:(p + 1) * nh, :]
        wsel = jnp.concatenate(
            [jnp.where(hmask, jnp.broadcast_to(wd[r:r + 1, :], (nh, LANES)), 0.0) for r in range(nh)], axis=1)
        vf = v_refs[p][...].reshape(-1, acc.shape[1]).astype(BF16)
        acc = acc + _dot(wsel.astype(BF16), vf)
    acc_ref[...] = acc

    @pl.when(c == pl.num_programs(1) - 1)
    def _():
        if with_self:
            lane = lax.broadcasted_iota(jnp.int32, (nh, LANES), 1)
            wcol = jnp.sum(jnp.where(hmask & (lane < nh), jnp.broadcast_to(wself_ref[...], (nh, LANES)), 0.0),
                           axis=1, keepdims=True)
            o_ref[...] = acc + wcol * vnew_ref[...]
        else:
            o_ref[...] = acc


def _dec_pv(w, cache, layer, page_table, *, nh, hd, wself=None, vnew=None):
    Bd, n_pages = page_table.shape
    npp = min(DEC_PAGES_PER_STEP, n_pages)
    with_self = wself is not None
    in_specs = ([pl.BlockSpec((None, npp * nh, LANES), lambda b, c, pt: (b, c, 0))]
                + _page_specs(layer, cache, npp))
    args = [w] + [cache] * npp
    if with_self:
        in_specs += [pl.BlockSpec((None, 1, LANES), lambda b, c, pt: (b, 0, 0)),
                     pl.BlockSpec((None, nh, hd), lambda b, c, pt: (b, 0, 0))]
        args += [wself, vnew]
    kern = functools.partial(_dec_pv_kernel, nh=nh, npp=npp, with_self=with_self)
    grid_spec = pltpu.PrefetchScalarGridSpec(
        num_scalar_prefetch=1, grid=(Bd, n_pages // npp), in_specs=in_specs,
        out_specs=pl.BlockSpec((None, nh, hd), lambda b, c, pt: (b, 0, 0)),
        scratch_shapes=[pltpu.VMEM((nh, hd), F32)],
    )
    return pl.pallas_call(
        kern, grid_spec=grid_spec,
        out_shape=jax.ShapeDtypeStruct((Bd, nh, hd), F32),
        compiler_params=_cparams(("parallel", "arbitrary"), 48), name="dec_pv",
    )(page_table, *args)


def _suffix_in_row(x, nh):
    lane = lax.broadcasted_iota(jnp.int32, x.shape, 1)
    y = x
    s = nh
    while s < LANES:
        y = y + jnp.where(lane + s < LANES, pltpu.roll(y, LANES - s, axis=1), 0.0)
        s *= 2
    return y


def _allreduce_in_row(x, nh, op):
    s = nh
    while s < LANES:
        x = op(x, pltpu.roll(x, s, axis=1))
        s *= 2
    return x


DEC_WEIGHT_SEQS = 8


def _dec_sb_weights_kernel(z_ref, u_ref, w_ref, *, nh):
    u = u_ref[...]
    for j in range(DEC_WEIGHT_SEQS):
        z = z_ref[j]
        sp = _softplus(z)
        in_row = _suffix_in_row(sp, nh)
        row_tot = _allreduce_in_row(sp, nh, jnp.add)
        later_rows = _dot_left_exact(u, row_tot)
        later = in_row - sp + later_rows
        w_ref[j] = jnp.exp(z - sp - later)


def _dec_sb_weights(z, ustrict, *, nh):
    Bd, R, _ = z.shape
    nb = DEC_WEIGHT_SEQS
    blk = pl.BlockSpec((nb, R, LANES), lambda b: (b, 0, 0))
    kern = functools.partial(_dec_sb_weights_kernel, nh=nh)
    return pl.pallas_call(
        kern, grid=(Bd // nb,),
        in_specs=[blk, _full(ustrict.shape)], out_specs=blk,
        out_shape=jax.ShapeDtypeStruct(z.shape, F32),
        compiler_params=_cparams(("parallel",), 32), name="dec_sb_weights",
    )(z, ustrict)


def _dec_fox_weights_kernel(pt_ref, z_ref, q_ref, kn_ref, lfn_ref, u_ref, *refs, nh, n_pages, scale):
    lf_refs = refs[:n_pages]
    w_ref, wself_ref = refs[n_pages:]
    lf = jnp.concatenate([r[...] for r in lf_refs], axis=0)
    in_row = _suffix_in_row(lf, nh)
    row_tot = _allreduce_in_row(lf, nh, jnp.add)
    later_rows = _dot_left_exact(u_ref[...], row_tot)
    decay = in_row - lf + later_rows + lfn_ref[...]
    logits = z_ref[...] + decay
    hmask = _head_mask(nh)
    s_col = jnp.sum(q_ref[...] * kn_ref[...], axis=1, keepdims=True) * scale
    s_lane = jnp.sum(jnp.where(hmask, jnp.broadcast_to(s_col, (nh, LANES)), 0.0), axis=0, keepdims=True)
    m = _allreduce_in_row(jnp.max(logits, axis=0, keepdims=True), nh, jnp.maximum)
    m = jnp.maximum(m, s_lane)
    p = jnp.exp(logits - m)
    p_self = jnp.exp(s_lane - m)
    denom = _allreduce_in_row(jnp.sum(p, axis=0, keepdims=True), nh, jnp.add) + p_self
    w_ref[...] = p / denom
    wself_ref[...] = p_self / denom


def _dec_fox_weights(z, q, kn, lfn_lane, ustrict, logf_dense, layer, page_table, *, nh, hd):
    Bd, n_pages = page_table.shape
    R = z.shape[1]
    lf_specs = [pl.BlockSpec((None, None, nh, LANES), (lambda p: lambda b, pt: (layer, pt[b, p], 0, 0))(p))
                for p in range(n_pages)]
    kern = functools.partial(_dec_fox_weights_kernel, nh=nh, n_pages=n_pages, scale=float(hd ** -0.5))
    grid_spec = pltpu.PrefetchScalarGridSpec(
        num_scalar_prefetch=1, grid=(Bd,),
        in_specs=[pl.BlockSpec((None, R, LANES), lambda b, pt: (b, 0, 0)),
                  pl.BlockSpec((None, nh, hd), lambda b, pt: (b, 0, 0)),
                  pl.BlockSpec((None, nh, hd), lambda b, pt: (b, 0, 0)),
                  pl.BlockSpec((None, 1, LANES), lambda b, pt: (b, 0, 0)),
                  pl.BlockSpec(ustrict.shape, lambda b, pt: (0, 0))] + lf_specs,
        out_specs=(pl.BlockSpec((None, R, LANES), lambda b, pt: (b, 0, 0)),
                   pl.BlockSpec((None, 1, LANES), lambda b, pt: (b, 0, 0))),
    )
    return pl.pallas_call(
        kern, grid_spec=grid_spec,
        out_shape=(jax.ShapeDtypeStruct(z.shape, F32), jax.ShapeDtypeStruct((Bd, 1, LANES), F32)),
        compiler_params=_cparams(("parallel",), 32), name="dec_fox_weights",
    )(page_table, z, q, kn, lfn_lane, ustrict, *([logf_dense] * n_pages))


def _lower_incl(n):
    i = jnp.arange(n)
    return (i[None, :] <= i[:, None]).astype(BF16)


def _upper_strict(n):
    i = jnp.arange(n)
    return (i[None, :] > i[:, None]).astype(BF16)


def _block_diag(n, blk):
    i = jnp.arange(n)
    return ((i[:, None] // blk) == (i[None, :] // blk)).astype(BF16)


def _bias_placement(nh, hd):
    e = jnp.zeros((3, nh, nh * LANES), F32)
    h = jnp.arange(nh)
    for p in range(3):
        e = e.at[p, h, h * LANES + hd + p].set(1.0)
    return e.astype(BF16)


def kernel(x_prompt, x_sample, state_hgrn, cache_sb_k, cache_sb_v, cache_fox_k, cache_fox_v, cache_fox_logf,
           page_table, norm_g, w_in_ab, w_out_ab, lb_param, out_norm_a, w_in_c, b_f_c, w_out_c,
           qk_norm_q, qk_norm_k):
    B, T, D = x_prompt.shape
    Bd = x_sample.shape[0]
    n_ab, _, H_A, DK_A, DV_A = state_hgrn.shape
    n_c = cache_fox_k.shape[0]
    pool, page_size, H_B, HD_B = cache_sb_k.shape[1:]
    H_C, HD_C = cache_fox_k.shape[3:]
    FA, WA, WB, WC = H_A * DK_A, H_A * DV_A, H_B * HD_B, H_C * HD_C
    depth = norm_g.shape[0]
    assert DK_A == DV_A == LANES and HD_B == HD_C == LANES // 2 and page_size == LANES
    assert w_in_ab.shape[2] == 2 * FA + 2 * WA + 4 * WB and w_in_c.shape[2] == 4 * WC + H_C

    tm = min(256, T)
    tc = min(512, T)
    tq_fox = min(FOX_TQ, T)
    assert T % tm == 0 and T % tc == 0 and tc % HGRN_CHUNK == 0 and Bd % HGRN_STEP_SEQS == 0
    assert T % tq_fox == 0 and tq_fox % tm == 0 and H_B % 2 == 0 and H_C % FOX_HEADS == 0
    nt = T // tm

    p = jax.nn.softmax(lb_param.astype(F32), axis=0)
    lower_bounds = jnp.cumsum(p, axis=0) - p[0]

    ltri_chunk = _lower_incl(HGRN_CHUNK)
    ltri_tm = _lower_incl(tm)
    ustrict_tm = _upper_strict(tm)
    bd = _block_diag(256, HD_C)
    emat = _bias_placement(H_C, HD_C)
    n_pages = page_table.shape[1]
    ustrict_sb = _upper_strict(n_pages * H_B)
    ustrict_fox = _upper_strict(n_pages * H_C)

    sbk, sbv, fxk, fxv = cache_sb_k, cache_sb_v, cache_fox_k, cache_fox_v
    fxlf = cache_fox_logf.reshape(n_c, pool, H_C, LANES)

    xs = x_sample.reshape(1, Bd, D)
    yp, ys = x_prompt, xs
    hgrn_p, hgrn_s, sbk_p, sbv_p, sbk_s, sbv_s = [], [], [], [], [], []
    fk_p, fv_p, fl_p, fk_s, fv_s, fl_s = [], [], [], [], [], []

    for l in range(depth):
        j = l // 2
        g_pre = norm_g[l].reshape(1, D)
        if l % 2 == 0:
            w_in = w_in_ab[j].astype(BF16)
            w_out = w_out_ab[j].astype(BF16)
            lb = lower_bounds[j].reshape(1, FA)
            g_out = out_norm_a[j].reshape(1, DV_A)
            qa, ka, lf, va, ga, gb, kbo, vbo, qb, kb, vbt = _inproj_ab(yp, g_pre, w_in, lb, fa=FA, wa=WA, wb=WB,
                                                                       hd_b=HD_B, tm=tm)
            oa, S = _hgrn_prompt(qa, ka, va, lf, ltri_chunk, nh=H_A, dk=DK_A, tc=tc)
            obt = _sb_prompt(qb, kb, vbt.reshape(B, nt, H_B, HD_B, tm), ustrict_tm, nh=H_B, hd=HD_B, tq=tm)
            yp = _outproj_ab(yp, oa, ga, obt, gb, g_out, w_out, nh=H_A, dv=DV_A, tm=tm, ob_transposed=True)
            hgrn_p.append(S)
            sbk_p.append(kbo.reshape(B, T, H_B, HD_B))
            sbv_p.append(vbo.reshape(B, T, H_B, HD_B))
            qa, ka, lf, va, ga, gb, kbo, vbo, qb, _, _ = _inproj_ab(ys, g_pre, w_in, lb, fa=FA, wa=WA, wb=WB,
                                                                    hd_b=HD_B, tm=Bd)
            oa, S = _hgrn_step(qa[0], ka[0], va[0], lf[0], state_hgrn[j], nh=H_A, dk=DK_A)
            qd = qb[0].astype(F32).reshape(Bd, H_B, HD_B)
            z = _dec_scores(qd, sbk, j, page_table, nh=H_B, hd=HD_B)
            w = _dec_sb_weights(z, ustrict_sb, nh=H_B)
            ob = _dec_pv(w, sbv, j, page_table, nh=H_B, hd=HD_B)
            ys = _outproj_ab(ys, oa[None], ga, ob.reshape(1, Bd, WB), gb, g_out, w_out, nh=H_A, dv=DV_A, tm=Bd,
                             ob_transposed=False)
            hgrn_s.append(S)
            sbk_s.append(kbo.reshape(Bd, 1, H_B, HD_B))
            sbv_s.append(vbo.reshape(Bd, 1, H_B, HD_B))
        else:
            w_main = w_in_c[j][:, :4 * WC].astype(BF16)
            w_f = jnp.pad(w_in_c[j][:, 4 * WC:], ((0, 0), (0, LANES - H_C))).astype(BF16)
            w_out = w_out_c[j].astype(BF16)
            b_f = b_f_c[j].reshape(1, H_C)
            gq = jnp.tile(qk_norm_q[j], H_C).reshape(1, WC)
            gk = jnp.tile(qk_norm_k[j], H_C).reshape(1, WC)
            qp, kn, v, gate, logf, vt = _inproj_c(yp, g_pre, w_main, w_f, b_f, gq, gk, bd, wc=WC, hd=HD_C, nh=H_C,
                                                  tm=tm, decode=False)
            kp, clast = _fox_prep(kn, logf, ltri_tm, emat, nh=H_C, hd=HD_C, tm=tm)
            bound = (HD_C * (HD_C ** -0.5) * LOG2E * FOX_BOUND_SLACK
                     * jnp.max(jnp.abs(qk_norm_q[j])) * jnp.max(jnp.abs(qk_norm_k[j]))).reshape(1).astype(F32)
            cend = jnp.transpose(clast[:, tq_fox // tm - 1::tq_fox // tm, 0, :], (0, 2, 1))
            ot = _fox_prompt(qp, kp, vt.reshape(B, nt, H_C, HD_C, tm), bound, cend, nh=H_C, hd=HD_C, tq=tq_fox, tm=tm)
            yp = _outproj_c(yp, ot, gate, w_out, tm=tm, o_transposed=True)
            fk_p.append(kn.reshape(B, T, H_C, HD_C))
            fv_p.append(v.reshape(B, T, H_C, HD_C))
            fl_p.append(logf)
            qn, kn, v, gate, logf = _inproj_c(ys, g_pre, w_main, w_f, b_f, gq, gk, bd, wc=WC, hd=HD_C, nh=H_C,
                                              tm=Bd, decode=True)
            qd = qn[0].reshape(Bd, H_C, HD_C)
            kd = kn[0].reshape(Bd, H_C, HD_C)
            vd = v[0].reshape(Bd, H_C, HD_C)
            lfn_lane = jnp.tile(logf[0], (1, LANES // H_C)).reshape(Bd, 1, LANES)
            z = _dec_scores(qd * (HD_C ** -0.5), fxk, j, page_table, nh=H_C, hd=HD_C)
            w, wself = _dec_fox_weights(z, qd, kd, lfn_lane, ustrict_fox, fxlf, j, page_table, nh=H_C, hd=HD_C)
            o = _dec_pv(w, fxv, j, page_table, nh=H_C, hd=HD_C, wself=wself, vnew=vd)
            ys = _outproj_c(ys, o.reshape(1, Bd, WC), gate, w_out, tm=Bd, o_transposed=False)
            fk_s.append(kd.reshape(Bd, 1, H_C, HD_C))
            fv_s.append(vd.reshape(Bd, 1, H_C, HD_C))
            fl_s.append(logf[0].reshape(Bd, 1, H_C))

    return (yp, ys.reshape(Bd, 1, D), jnp.stack(hgrn_p), jnp.stack(hgrn_s), jnp.stack(sbk_p), jnp.stack(sbv_p),
            jnp.stack(sbk_s), jnp.stack(sbv_s), jnp.stack(fk_p), jnp.stack(fv_p), jnp.stack(fl_p),
            jnp.stack(fk_s), jnp.stack(fv_s), jnp.stack(fl_s))
```

```python
import functools

import jax
import jax.numpy as jnp
from jax import lax
from jax.experimental import pallas as pl
from jax.experimental.pallas import tpu as pltpu

F32 = jnp.float32
BF16 = jnp.bfloat16
EPS = 1e-6
NEG_BIG = -1e30
LANES = 128
MIB = 1024 * 1024
NT_DIMS = (((1,), (1,)), ((), ()))


def _cparams(semantics, vmem_mib):
    return pltpu.CompilerParams(dimension_semantics=semantics, vmem_limit_bytes=vmem_mib * MIB)


def _dot(a, b):
    return jnp.dot(a, b, preferred_element_type=F32)


def _dot_nt(a, b):
    return lax.dot_general(a, b, NT_DIMS, preferred_element_type=F32)


def _split2(x):
    hi = x.astype(BF16)
    lo = (x - hi.astype(F32)).astype(BF16)
    return hi, lo


def _split3(x):
    hi = x.astype(BF16)
    r = x - hi.astype(F32)
    mid = r.astype(BF16)
    lo = (r - mid.astype(F32)).astype(BF16)
    return hi, mid, lo


def _dot_left_exact(mat01, x):
    hi, mid, lo = _split3(x)
    return _dot(mat01, hi) + _dot(mat01, mid) + _dot(mat01, lo)


def _sigmoid(x):
    return 1.0 / (1.0 + jnp.exp(-x))


def _softplus(x):
    return jnp.maximum(x, 0.0) + jnp.log(1.0 + jnp.exp(-jnp.abs(x)))


def _rms(x, g):
    return x * lax.rsqrt(jnp.mean(x * x, axis=-1, keepdims=True) + EPS) * g


def _full(shape):
    nd = len(shape)
    return pl.BlockSpec(shape, lambda *_: (0,) * nd)


def _inproj_ab_kernel(x_ref, g_ref, w_ref, lb_ref,
                      qa_ref, ka_ref, lf_ref, va_ref, ga_ref, gb_ref, kbo_ref, vbo_ref,
                      qb_ref, kb_ref, vbt_ref, *, fa, wa, wb, sb_scale):
    xn = _rms(x_ref[...], g_ref[...]).astype(BF16)
    offs = [0]

    def proj(n):
        lo = offs[0]
        offs[0] = lo + n
        return _dot(xn, w_ref[:, lo:lo + n])

    qa = proj(fa)
    za = proj(fa)
    ia = proj(wa)
    ga = proj(wa)
    qb = proj(wb)
    kb = proj(wb)
    vb = proj(wb)
    gb = proj(wb)
    lb = lb_ref[...]
    lf_ref[...] = jnp.log(lb + (1.0 - lb) * _sigmoid(za))
    ka_ref[...] = (1.0 - lb) * _sigmoid(-za)
    qa_ref[...] = qa * _sigmoid(qa)
    va_ref[...] = ia
    ga_ref[...] = ga * _sigmoid(ga)
    gb_ref[...] = gb * _sigmoid(gb)
    kbo_ref[...] = kb
    vbo_ref[...] = vb
    qb_ref[...] = (qb * sb_scale).astype(BF16)
    kb_ref[...] = kb.astype(BF16)
    vbt_ref[...] = vb.T.astype(BF16)


def _inproj_ab(x, g, w_bf16, lb, *, fa, wa, wb, hd_b, tm):
    B, T, D = x.shape
    nt = T // tm
    cols = w_bf16.shape[1]
    row = lambda n: pl.BlockSpec((None, tm, n), lambda b, t: (b, t, 0))
    f32o = lambda n: jax.ShapeDtypeStruct((B, T, n), F32)
    out_shape = (f32o(fa), f32o(fa), f32o(fa), f32o(wa), f32o(wa), f32o(wb), f32o(wb), f32o(wb),
                 jax.ShapeDtypeStruct((B, T, wb), BF16), jax.ShapeDtypeStruct((B, T, wb), BF16),
                 jax.ShapeDtypeStruct((B, nt, wb, tm), BF16))
    out_specs = (row(fa), row(fa), row(fa), row(wa), row(wa), row(wb), row(wb), row(wb), row(wb), row(wb),
                 pl.BlockSpec((None, None, wb, tm), lambda b, t: (b, t, 0, 0)))
    kern = functools.partial(_inproj_ab_kernel, fa=fa, wa=wa, wb=wb, sb_scale=float(hd_b ** -0.5))
    return pl.pallas_call(
        kern, grid=(B, nt),
        in_specs=[row(D), _full((1, D)), _full((D, cols)), _full((1, fa))],
        out_specs=out_specs, out_shape=out_shape,
        compiler_params=_cparams(("parallel", "parallel"), 48), name="inproj_ab",
    )(x, g, w_bf16, lb)


def _outproj_ab_kernel(x_ref, oa_ref, ga_ref, ob_ref, gb_ref, gout_ref, w_ref, y_ref, *, nh, dv, ob_transposed):
    oa = oa_ref[...]
    gout = gout_ref[...]
    parts = [_rms(oa[:, h * dv:(h + 1) * dv], gout) for h in range(nh)]
    oa_n = jnp.concatenate(parts, axis=1) * ga_ref[...]
    ob = ob_ref[...]
    if ob_transposed:
        ob = ob.T
    ob = ob * gb_ref[...]
    cat = jnp.concatenate([oa_n, ob], axis=1).astype(BF16)
    y_ref[...] = x_ref[...] + _dot(cat, w_ref[...])


def _outproj_ab(x, oa, ga, ob, gb, gout, w_bf16, *, nh, dv, tm, ob_transposed):
    B, T, D = x.shape
    nt = T // tm
    wa = oa.shape[-1]
    wb = gb.shape[-1]
    row = lambda n: pl.BlockSpec((None, tm, n), lambda b, t: (b, t, 0))
    if ob_transposed:
        ob_spec = pl.BlockSpec((None, wb, tm), lambda b, t: (b, 0, t))
    else:
        ob_spec = row(wb)
    kern = functools.partial(_outproj_ab_kernel, nh=nh, dv=dv, ob_transposed=ob_transposed)
    return pl.pallas_call(
        kern, grid=(B, nt),
        in_specs=[row(D), row(wa), row(wa), ob_spec, row(wb), _full((1, dv)), _full((wa + wb, D))],
        out_specs=row(D), out_shape=jax.ShapeDtypeStruct((B, T, D), F32),
        compiler_params=_cparams(("parallel", "parallel"), 40), name="outproj_ab",
    )(x, oa, ga, ob, gb, gout, w_bf16)


def _inproj_c_kernel(x_ref, g_ref, w_ref, wf_ref, bf_ref, gq_ref, gk_ref, bd_ref, *out_refs,
                     wc, hd, nh, nb, fox_scale, decode):
    xn = _rms(x_ref[...], g_ref[...]).astype(BF16)
    bd = bd_ref[...]
    inv_hd = 1.0 / hd

    def normed(base, gfull_ref):
        outs = []
        for c in range(wc // nb):
            y = _dot(xn, w_ref[:, base + c * nb: base + (c + 1) * nb])
            hi, lo = _split2(y * y)
            ms = (_dot(hi, bd) + _dot(lo, bd)) * inv_hd
            outs.append(y * lax.rsqrt(ms + EPS) * gfull_ref[:, c * nb:(c + 1) * nb])
        return jnp.concatenate(outs, axis=1)

    qn = normed(0, gq_ref)
    kn = normed(wc, gk_ref)
    v = _dot(xn, w_ref[:, 2 * wc:3 * wc])
    gt = _dot(xn, w_ref[:, 3 * wc:4 * wc])
    f = _dot(xn, wf_ref[...])[:, :nh] + bf_ref[...]
    logf = -_softplus(-f)
    if decode:
        qn_ref, k_ref, v_ref, gate_ref, lf_ref = out_refs
        qn_ref[...] = qn
    else:
        qp_ref, k_ref, v_ref, gate_ref, lf_ref, vt_ref = out_refs
        lane = lax.broadcasted_iota(jnp.int32, (qn.shape[0], LANES), 1)
        ones_cols = jnp.where((lane >= hd) & (lane < hd + 3), 1.0, 0.0)
        for h in range(nh):
            pair = qn[:, (h // 2) * LANES:(h // 2 + 1) * LANES]
            if h % 2 == 1:
                pair = pltpu.roll(pair, hd, axis=1)
            qp_ref[h] = jnp.where(lane < hd, pair * fox_scale, ones_cols).astype(BF16)
        vt_ref[...] = v.T.astype(BF16)
    k_ref[...] = kn
    v_ref[...] = v
    gate_ref[...] = gt * _sigmoid(gt)
    lf_ref[...] = logf


def _inproj_c(x, g, w_bf16, wf_bf16, bf, gq_full, gk_full, bd, *, wc, hd, nh, tm, decode):
    B, T, D = x.shape
    nt = T // tm
    nb = bd.shape[0]
    row = lambda n: pl.BlockSpec((None, tm, n), lambda b, t: (b, t, 0))
    f32o = lambda n: jax.ShapeDtypeStruct((B, T, n), F32)
    common_shapes = (f32o(wc), f32o(wc), f32o(wc), f32o(nh))
    common_specs = (row(wc), row(wc), row(wc), row(nh))
    if decode:
        out_shape = (f32o(wc),) + common_shapes
        out_specs = (row(wc),) + common_specs
    else:
        out_shape = ((jax.ShapeDtypeStruct((B, nh, T, LANES), BF16),) + common_shapes
                     + (jax.ShapeDtypeStruct((B, nt, wc, tm), BF16),))
        out_specs = ((pl.BlockSpec((None, nh, tm, LANES), lambda b, t: (b, 0, t, 0)),) + common_specs
                     + (pl.BlockSpec((None, None, wc, tm), lambda b, t: (b, t, 0, 0)),))
    kern = functools.partial(_inproj_c_kernel, wc=wc, hd=hd, nh=nh, nb=nb, fox_scale=float(hd ** -0.5 * LOG2E), decode=decode)
    return pl.pallas_call(
        kern, grid=(B, nt),
        in_specs=[row(D), _full((1, D)), _full((D, 4 * wc)), _full((D, LANES)), _full((1, nh)),
                  _full((1, wc)), _full((1, wc)), _full((nb, nb))],
        out_specs=out_specs, out_shape=out_shape,
        compiler_params=_cparams(("parallel", "parallel"), 48), name="inproj_c",
    )(x, g, w_bf16, wf_bf16, bf, gq_full, gk_full, bd)


def _outproj_c_kernel(x_ref, o_ref, gate_ref, w_ref, y_ref, *, o_transposed):
    o = o_ref[...]
    if o_transposed:
        o = o.T
    y_ref[...] = x_ref[...] + _dot((o * gate_ref[...]).astype(BF16), w_ref[...])


def _outproj_c(x, o, gate, w_bf16, *, tm, o_transposed):
    B, T, D = x.shape
    nt = T // tm
    wc = gate.shape[-1]
    row = lambda n: pl.BlockSpec((None, tm, n), lambda b, t: (b, t, 0))
    o_spec = pl.BlockSpec((None, wc, tm), lambda b, t: (b, 0, t)) if o_transposed else row(wc)
    kern = functools.partial(_outproj_c_kernel, o_transposed=o_transposed)
    return pl.pallas_call(
        kern, grid=(B, nt),
        in_specs=[row(D), o_spec, row(wc), _full((wc, D))],
        out_specs=row(D), out_shape=jax.ShapeDtypeStruct((B, T, D), F32),
        compiler_params=_cparams(("parallel", "parallel"), 40), name="outproj_c",
    )(x, o, gate, w_bf16)


HGRN_CHUNK = 128
HGRN_SUB = 16
HGRN_EXP_CLAMP = 60.0


def _hgrn_prompt_kernel(q_ref, k_ref, v_ref, g_ref, ltri_ref, o_ref, s_ref, st_ref, *, nh, dk, tc):
    t = pl.program_id(1)

    @pl.when(t == 0)
    def _():
        st_ref[...] = jnp.zeros_like(st_ref)

    C, SB = HGRN_CHUNK, HGRN_SUB
    ltri = ltri_ref[...]

    def chunk(c, carry):
        r0 = pl.multiple_of(c * C, C)
        for h in range(nh):
            sl = slice(h * dk, (h + 1) * dk)
            q = q_ref[pl.ds(r0, C), sl]
            k = k_ref[pl.ds(r0, C), sl]
            v = v_ref[pl.ds(r0, C), sl]
            g = g_ref[pl.ds(r0, C), sl]
            G = _dot_left_exact(ltri, g)
            st = st_ref[h]
            o = _dot_nt((q * jnp.exp(G)).astype(BF16), st.astype(BF16))
            vb = v.astype(BF16)
            outs = []
            for i in range(C // SB):
                a, n = i * SB, (i + 1) * SB
                r = G[a - 1:a, :] if i > 0 else jnp.zeros((1, dk), F32)
                qt = (q[a:n] * jnp.exp(G[a:n] - r)).astype(BF16)
                kt = (k[:n] * jnp.exp(jnp.minimum(r - G[:n], HGRN_EXP_CLAMP))).astype(BF16)
                sc = _dot_nt(qt, kt)
                row = lax.broadcasted_iota(jnp.int32, (SB, n), 0) + a
                col = lax.broadcasted_iota(jnp.int32, (SB, n), 1)
                sc = jnp.where(col <= row, sc, 0.0)
                outs.append(_dot(sc.astype(BF16), vb[:n]))
            o_ref[pl.ds(r0, C), sl] = o + jnp.concatenate(outs, axis=0)
            gend = G[C - 1:C, :]
            kbar = (k * jnp.exp(gend - G)).astype(BF16)
            st_ref[h] = st * jnp.exp(gend) + _dot(v.T.astype(BF16), kbar)
        return carry

    lax.fori_loop(0, tc // C, chunk, 0)

    @pl.when(t == pl.num_programs(1) - 1)
    def _():
        for h in range(nh):
            s_ref[h] = st_ref[h].T


def _hgrn_prompt(q, k, v, g, ltri, *, nh, dk, tc):
    B, T, W = q.shape
    nt = T // tc
    row = pl.BlockSpec((None, tc, W), lambda b, t: (b, t, 0))
    kern = functools.partial(_hgrn_prompt_kernel, nh=nh, dk=dk, tc=tc)
    return pl.pallas_call(
        kern, grid=(B, nt),
        in_specs=[row, row, row, row, _full(ltri.shape)],
        out_specs=(row, pl.BlockSpec((None, nh, dk, dk), lambda b, t: (b, 0, 0, 0))),
        out_shape=(jax.ShapeDtypeStruct((B, T, W), F32), jax.ShapeDtypeStruct((B, nh, dk, dk), F32)),
        scratch_shapes=[pltpu.VMEM((nh, dk, dk), F32)],
        compiler_params=_cparams(("parallel", "arbitrary"), 32), name="hgrn_prompt",
    )(q, k, v, g, ltri)


HGRN_STEP_SEQS = 8


def _hgrn_step_kernel(q_ref, k_ref, v_ref, g_ref, s_ref, o_ref, so_ref, *, nh, dk):
    nb = HGRN_STEP_SEQS
    pad = jnp.zeros((dk - nb, dk), F32)

    def columns(x8):
        return jnp.concatenate([x8, pad], axis=0).T

    for h in range(nh):
        sl = slice(h * dk, (h + 1) * dk)
        egc = columns(jnp.exp(g_ref[:, sl]))
        kc = columns(k_ref[:, sl])
        qc = columns(q_ref[:, sl])
        for j in range(nb):
            s_new = egc[:, j:j + 1] * s_ref[j, h] + kc[:, j:j + 1] * v_ref[j:j + 1, sl]
            so_ref[j, h] = s_new
            o_ref[j:j + 1, sl] = jnp.sum(qc[:, j:j + 1] * s_new, axis=0, keepdims=True)


def _hgrn_step(q, k, v, g, s0, *, nh, dk):
    Bd, W = q.shape
    nb = HGRN_STEP_SEQS
    row = pl.BlockSpec((nb, W), lambda b: (b, 0))
    st = pl.BlockSpec((nb, nh, dk, dk), lambda b: (b, 0, 0, 0))
    kern = functools.partial(_hgrn_step_kernel, nh=nh, dk=dk)
    return pl.pallas_call(
        kern, grid=(Bd // nb,),
        in_specs=[row, row, row, row, st],
        out_specs=(row, st),
        out_shape=(jax.ShapeDtypeStruct((Bd, W), F32), jax.ShapeDtypeStruct(s0.shape, F32)),
        compiler_params=_cparams(("parallel",), 32), name="hgrn_step",
    )(q, k, v, g, s0)


SB_SKIP = 105.0
FOX_SKIP = 152.0
FOX_TQ = 512
FOX_HEADS = 2
LOG2E = 1.4426950408889634
FOX_BOUND_SLACK = 1.01


def _sb_prompt_kernel(q_ref, k_ref, vt_ref, u_ref, o_ref, *, tq, hd):
    i = pl.program_id(2)
    lane = lax.broadcasted_iota(jnp.int32, (tq, LANES), 1)
    q2 = q_ref[...]
    zero = jnp.zeros_like(q2)
    qs = (jnp.where(lane < hd, q2, zero), jnp.where(lane >= hd, q2, zero))
    u = u_ref[...]
    srow = lax.broadcasted_iota(jnp.int32, (tq, tq), 0)
    tcol = lax.broadcasted_iota(jnp.int32, (tq, tq), 1)
    causal = srow < tcol

    def block(j, state, masked):
        kb = k_ref[pl.ds(pl.multiple_of(j * tq, tq), tq), :]
        out = []
        for g in range(2):
            carry, acc = state[g]
            z = _dot_nt(kb, qs[g])
            sp = _softplus(z)
            if masked:
                sp = jnp.where(causal, sp, 0.0)
            hi, lo = _split2(sp)
            later = _dot(u, hi) + _dot(u, lo)
            w = jnp.exp(z - sp - later - carry)
            if masked:
                w = jnp.where(causal, w, 0.0)
            acc = acc + _dot(vt_ref[j, g], w.astype(BF16))
            carry = carry + jnp.sum(sp, axis=0, keepdims=True)
            out.append((carry, acc))
        return tuple(out)

    init = (jnp.zeros((1, tq), F32), jnp.zeros((hd, tq), F32))
    state = block(i, (init, init), True)

    def cond(st):
        j, state = st
        smallest = jnp.minimum(jnp.min(state[0][0]), jnp.min(state[1][0]))
        return (j >= 0) & (smallest <= SB_SKIP)

    def body(st):
        j, state = st
        return j - 1, block(j, state, False)

    _, state = lax.while_loop(cond, body, (i - 1, state))
    o_ref[...] = jnp.concatenate([state[0][1], state[1][1]], axis=0)


def _sb_prompt(qb, kb, vbt, ustrict, *, nh, hd, tq):
    B, T, W = qb.shape
    nt = T // tq
    kern = functools.partial(_sb_prompt_kernel, tq=tq, hd=hd)
    return pl.pallas_call(
        kern, grid=(B, nh // 2, nt),
        in_specs=[pl.BlockSpec((None, tq, LANES), lambda b, h, i: (b, i, h)),
                  pl.BlockSpec((None, T, LANES), lambda b, h, i: (b, 0, h)),
                  pl.BlockSpec((None, nt, 2, hd, tq), lambda b, h, i: (b, 0, h, 0, 0)),
                  _full((tq, tq))],
        out_specs=pl.BlockSpec((None, 2 * hd, tq), lambda b, h, i: (b, h, i)),
        out_shape=jax.ShapeDtypeStruct((B, W, T), F32),
        compiler_params=_cparams(("parallel", "parallel", "arbitrary"), 40), name="sb_prompt",
    )(qb, kb, vbt, ustrict)


def _fox_prep_kernel(kn_ref, lf_ref, ltri_ref, e_ref, kp_ref, clast_ref, carry_ref, *, nh, hd):
    t = pl.program_id(1)

    @pl.when(t == 0)
    def _():
        carry_ref[...] = jnp.zeros_like(carry_ref)

    c = carry_ref[...] + _dot_left_exact(ltri_ref[...], lf_ref[...])
    tm = c.shape[0]
    carry_ref[...] = c[tm - 1:tm, :]
    c2 = c * LOG2E
    clast_ref[...] = c2[tm - 1:tm, :]
    hi, mid, lo = _split3(-c2)
    aug = _dot(hi, e_ref[0]) + _dot(mid, e_ref[1]) + _dot(lo, e_ref[2])
    kn = kn_ref[...]
    lane = lax.broadcasted_iota(jnp.int32, (tm, LANES), 1)
    for h in range(nh):
        pair = kn[:, (h // 2) * LANES:(h // 2 + 1) * LANES]
        if h % 2 == 1:
            pair = pltpu.roll(pair, hd, axis=1)
        kp_ref[h] = jnp.where(lane < hd, pair, aug[:, h * LANES:(h + 1) * LANES]).astype(BF16)


def _fox_prep(kn, logf, ltri, emat, *, nh, hd, tm):
    B, T, W = kn.shape
    nt = T // tm
    kern = functools.partial(_fox_prep_kernel, nh=nh, hd=hd)
    return pl.pallas_call(
        kern, grid=(B, nt),
        in_specs=[pl.BlockSpec((None, tm, W), lambda b, t: (b, t, 0)),
                  pl.BlockSpec((None, tm, nh), lambda b, t: (b, t, 0)),
                  _full(ltri.shape), _full(emat.shape)],
        out_specs=(pl.BlockSpec((None, nh, tm, LANES), lambda b, t: (b, 0, t, 0)),
                   pl.BlockSpec((None, None, 1, nh), lambda b, t: (b, t, 0, 0))),
        out_shape=(jax.ShapeDtypeStruct((B, nh, T, LANES), BF16), jax.ShapeDtypeStruct((B, nt, 1, nh), F32)),
        scratch_shapes=[pltpu.VMEM((1, nh), F32)],
        compiler_params=_cparams(("parallel", "arbitrary"), 40), name="fox_prep",
    )(kn, logf, ltri, emat)


def _fox_prompt_kernel(bound_ref, cend_ref, q_ref, k_ref, vt_ref, o_ref, *, tq, tm, hd, G):
    b = pl.program_id(0)
    hg = pl.program_id(1)
    i = pl.program_id(2)
    nsub = tq // tm
    srow = lax.broadcasted_iota(jnp.int32, (tq, tq), 0)
    tcol = lax.broadcasted_iota(jnp.int32, (tq, tq), 1)
    causal = srow <= tcol

    def block(j, state, masked):
        out = []
        for g in range(G):
            m, l, acc = state[g]
            s = _dot_nt(k_ref[g, j], q_ref[g])
            if masked:
                s = jnp.where(causal, s, NEG_BIG)
            m_new = jnp.maximum(m, jnp.max(s, axis=0, keepdims=True))
            alpha = jnp.exp2(m - m_new)
            p = jnp.exp2(s - m_new)
            l = alpha * l + jnp.sum(p, axis=0, keepdims=True)
            pb = p.astype(BF16)
            pv = _dot(vt_ref[j * nsub, g], pb[:tm])
            for r in range(1, nsub):
                pv = pv + _dot(vt_ref[j * nsub + r, g], pb[r * tm:(r + 1) * tm])
            out.append((m_new, l, alpha * acc + pv))
        return tuple(out)

    init = (jnp.full((1, tq), NEG_BIG, F32), jnp.zeros((1, tq), F32), jnp.zeros((hd, tq), F32))
    state = block(i, (init,) * G, True)
    bound = bound_ref[0]

    def cond(st):
        j, state = st
        jj = jnp.maximum(j, 0)
        alive = None
        for g in range(G):
            ub = bound - cend_ref[b, hg * G + g, jj]
            a = ub - jnp.min(state[g][0]) > -FOX_SKIP
            alive = a if alive is None else (alive | a)
        return (j >= 0) & alive

    def body(st):
        j, state = st
        return j - 1, block(j, state, False)

    _, state = lax.while_loop(cond, body, (i - 1, state))
    o_ref[...] = jnp.concatenate([acc / l for (_, l, acc) in state], axis=0)


def _fox_prompt(qp, kp, vt, bound, cend, *, nh, hd, tq, tm):
    B, _, T, _ = qp.shape
    nt = T // tq
    G = FOX_HEADS
    kp5 = kp.reshape(B, nh, nt, tq, LANES)
    kern = functools.partial(_fox_prompt_kernel, tq=tq, tm=tm, hd=hd, G=G)
    grid_spec = pltpu.PrefetchScalarGridSpec(
        num_scalar_prefetch=2, grid=(B, nh // G, nt),
        in_specs=[pl.BlockSpec((None, G, tq, LANES), lambda b, h, i, *_: (b, h, i, 0)),
                  pl.BlockSpec((None, G, nt, tq, LANES), lambda b, h, i, *_: (b, h, 0, 0, 0)),
                  pl.BlockSpec((None, T // tm, G, hd, tm), lambda b, h, i, *_: (b, 0, h, 0, 0))],
        out_specs=pl.BlockSpec((None, G * hd, tq), lambda b, h, i, *_: (b, h, i)),
    )
    return pl.pallas_call(
        kern, grid_spec=grid_spec,
        out_shape=jax.ShapeDtypeStruct((B, nh * hd, T), F32),
        compiler_params=_cparams(("parallel", "parallel", "arbitrary"), 48), name="fox_prompt",
    )(bound, cend, qp, kp5, vt)


DEC_PAGES_PER_STEP = 16


def _head_block_mask(nh, hd):
    sub = lax.broadcasted_iota(jnp.int32, (nh, nh * hd), 0)
    lane = lax.broadcasted_iota(jnp.int32, (nh, nh * hd), 1)
    return (lane // hd) == sub


def _dec_scores_kernel(pt_ref, q_ref, *refs, nh, hd, npp):
    k_refs, z_ref = refs[:npp], refs[npp]
    mask = _head_block_mask(nh, hd)
    qblk = jnp.where(mask, jnp.broadcast_to(q_ref[...], mask.shape), 0.0).astype(BF16)
    for p in range(npp):
        kt = k_refs[p][...].reshape(nh * hd, LANES).astype(BF16)
        z_ref[p] = _dot(qblk, kt)


def _page_specs(layer, cache_t, npp):
    blk = (None, None) + tuple(cache_t.shape[2:])

    def mk(p):
        return pl.BlockSpec(blk, lambda b, c, pt: (layer, pt[b, c * npp + p], 0, 0, 0))
    return [mk(p) for p in range(npp)]


def _dec_scores(q, cache_t, layer, page_table, *, nh, hd):
    Bd, n_pages = page_table.shape
    npp = min(DEC_PAGES_PER_STEP, n_pages)
    kern = functools.partial(_dec_scores_kernel, nh=nh, hd=hd, npp=npp)
    grid_spec = pltpu.PrefetchScalarGridSpec(
        num_scalar_prefetch=1, grid=(Bd, n_pages // npp),
        in_specs=[pl.BlockSpec((None, 1, nh * hd), lambda b, c, pt: (b, 0, 0))] + _page_specs(layer, cache_t, npp),
        out_specs=pl.BlockSpec((None, npp, nh, LANES), lambda b, c, pt: (b, c, 0, 0)),
    )
    return pl.pallas_call(
        kern, grid_spec=grid_spec,
        out_shape=jax.ShapeDtypeStruct((Bd, n_pages, nh, LANES), F32),
        compiler_params=_cparams(("parallel", "parallel"), 48), name="dec_scores",
    )(page_table, q, *([cache_t] * npp))


def _dec_pv_kernel(pt_ref, w_ref, *refs, nh, hd, npp, with_self):
    v_refs = refs[:npp]
    rest = refs[npp:]
    if with_self:
        wself_ref, vnew_ref, o_ref, acc_ref = rest
    else:
        o_ref, acc_ref = rest
    c = pl.program_id(1)

    @pl.when(c == 0)
    def _():
        acc_ref[...] = jnp.zeros_like(acc_ref)

    acc = acc_ref[...]
    for p in range(npp):
        vt = v_refs[p][...].reshape(nh * hd, LANES).astype(BF16)
        acc = acc + _dot_nt(w_ref[p].astype(BF16), vt)
    acc_ref[...] = acc

    @pl.when(c == pl.num_programs(1) - 1)
    def _():
        mask = _head_block_mask(nh, hd)
        o = jnp.sum(jnp.where(mask, acc, 0.0), axis=0, keepdims=True)
        if with_self:
            wlane = jnp.sum(jnp.where(mask, jnp.broadcast_to(wself_ref[...], mask.shape), 0.0), axis=0, keepdims=True)
            o = o + wlane * vnew_ref[...]
        o_ref[...] = o


def _dec_pv(w, cache_t, layer, page_table, *, nh, hd, wself=None, vnew=None):
    Bd, n_pages = page_table.shape
    npp = min(DEC_PAGES_PER_STEP, n_pages)
    with_self = wself is not None
    row = pl.BlockSpec((None, 1, nh * hd), lambda b, c, pt: (b, 0, 0))
    in_specs = ([pl.BlockSpec((None, npp, nh, LANES), lambda b, c, pt: (b, c, 0, 0))]
                + _page_specs(layer, cache_t, npp))
    args = [w] + [cache_t] * npp
    if with_self:
        in_specs += [pl.BlockSpec((None, nh, 1), lambda b, c, pt: (b, 0, 0)), row]
        args += [wself, vnew]
    kern = functools.partial(_dec_pv_kernel, nh=nh, hd=hd, npp=npp, with_self=with_self)
    grid_spec = pltpu.PrefetchScalarGridSpec(
        num_scalar_prefetch=1, grid=(Bd, n_pages // npp), in_specs=in_specs, out_specs=row,
        scratch_shapes=[pltpu.VMEM((nh, nh * hd), F32)],
    )
    return pl.pallas_call(
        kern, grid_spec=grid_spec,
        out_shape=jax.ShapeDtypeStruct((Bd, 1, nh * hd), F32),
        compiler_params=_cparams(("parallel", "arbitrary"), 48), name="dec_pv",
    )(page_table, *args)


def _later_tokens(x, lstrict, pmat):
    hi, mid, lo = _split3(x)
    in_page = _dot(hi, lstrict) + _dot(mid, lstrict) + _dot(lo, lstrict)
    page_tot = jnp.broadcast_to(jnp.sum(x, axis=1, keepdims=True), x.shape)
    return in_page + _dot_left_exact(pmat, page_tot)


DEC_WEIGHT_SEQS = 8


def _dec_sb_weights_kernel(z_ref, lstrict_ref, pmat_ref, w_ref, *, n_pages, nh):
    for j in range(DEC_WEIGHT_SEQS):
        z = z_ref[j].reshape(n_pages * nh, LANES)
        sp = _softplus(z)
        later = _later_tokens(sp, lstrict_ref[...], pmat_ref[...])
        w_ref[j] = jnp.exp(z - sp - later).reshape(n_pages, nh, LANES)


def _dec_sb_weights(z, lstrict, pmat):
    Bd, n_pages, nh, _ = z.shape
    nb = DEC_WEIGHT_SEQS
    blk = pl.BlockSpec((nb, n_pages, nh, LANES), lambda b: (b, 0, 0, 0))
    kern = functools.partial(_dec_sb_weights_kernel, n_pages=n_pages, nh=nh)
    return pl.pallas_call(
        kern, grid=(Bd // nb,),
        in_specs=[blk, _full(lstrict.shape), _full(pmat.shape)], out_specs=blk,
        out_shape=jax.ShapeDtypeStruct(z.shape, F32),
        compiler_params=_cparams(("parallel",), 32), name="dec_sb_weights",
    )(z, lstrict, pmat)


def _dec_fox_weights_kernel(pt_ref, z_ref, q_ref, kn_ref, lfn_ref, lstrict_ref, pmat_ref, *refs,
                            nh, hd, n_pages, scale):
    lf_refs = refs[:n_pages]
    w_ref, wself_ref = refs[n_pages:]
    R = n_pages * nh
    lf = jnp.concatenate([r[...] for r in lf_refs], axis=0)
    over_pages = lambda col: jnp.concatenate([col] * n_pages, axis=0)
    decay = _later_tokens(lf, lstrict_ref[...], pmat_ref[...]) + over_pages(lfn_ref[...])
    logits = z_ref[...].reshape(R, LANES) + decay
    mask = _head_block_mask(nh, hd)
    qk = jnp.broadcast_to(q_ref[...] * kn_ref[...], mask.shape)
    s_self = jnp.sum(jnp.where(mask, qk, 0.0), axis=1, keepdims=True) * scale
    row_max = jnp.max(logits, axis=1, keepdims=True)
    m = s_self
    for p in range(n_pages):
        m = jnp.maximum(m, row_max[p * nh:(p + 1) * nh])
    pexp = jnp.exp(logits - over_pages(m))
    p_self = jnp.exp(s_self - m)
    row_sum = jnp.sum(pexp, axis=1, keepdims=True)
    denom = p_self
    for p in range(n_pages):
        denom = denom + row_sum[p * nh:(p + 1) * nh]
    w_ref[...] = (pexp / over_pages(denom)).reshape(n_pages, nh, LANES)
    wself_ref[...] = p_self / denom


def _dec_fox_weights(z, q, kn, lfn, lstrict, pmat, logf_t, layer, page_table, *, nh, hd):
    Bd, n_pages = page_table.shape
    lf_specs = [pl.BlockSpec((None, None, nh, LANES), (lambda p: lambda b, pt: (layer, pt[b, p], 0, 0))(p))
                for p in range(n_pages)]
    kern = functools.partial(_dec_fox_weights_kernel, nh=nh, hd=hd, n_pages=n_pages, scale=float(hd ** -0.5))
    zblk = pl.BlockSpec((None, n_pages, nh, LANES), lambda b, pt: (b, 0, 0, 0))
    row = pl.BlockSpec((None, 1, nh * hd), lambda b, pt: (b, 0, 0))
    col = pl.BlockSpec((None, nh, 1), lambda b, pt: (b, 0, 0))
    grid_spec = pltpu.PrefetchScalarGridSpec(
        num_scalar_prefetch=1, grid=(Bd,),
        in_specs=[zblk, row, row, col,
                  pl.BlockSpec(lstrict.shape, lambda b, pt: (0, 0)),
                  pl.BlockSpec(pmat.shape, lambda b, pt: (0, 0))] + lf_specs,
        out_specs=(zblk, col),
    )
    return pl.pallas_call(
        kern, grid_spec=grid_spec,
        out_shape=(jax.ShapeDtypeStruct(z.shape, F32), jax.ShapeDtypeStruct((Bd, nh, 1), F32)),
        compiler_params=_cparams(("parallel",), 32), name="dec_fox_weights",
    )(page_table, z, q, kn, lfn, lstrict, pmat, *([logf_t] * n_pages))


def _lower_incl(n):
    i = jnp.arange(n)
    return (i[None, :] <= i[:, None]).astype(BF16)


def _upper_strict(n):
    i = jnp.arange(n)
    return (i[None, :] > i[:, None]).astype(BF16)


def _lower_strict(n):
    i = jnp.arange(n)
    return (i[:, None] > i[None, :]).astype(BF16)


def _later_pages(n_pages, nh):
    r = jnp.arange(n_pages * nh)
    return (((r[None, :] % nh) == (r[:, None] % nh)) & (r[None, :] > r[:, None])).astype(BF16)


def _block_diag(n, blk):
    i = jnp.arange(n)
    return ((i[:, None] // blk) == (i[None, :] // blk)).astype(BF16)


def _bias_placement(nh, hd):
    e = jnp.zeros((3, nh, nh * LANES), F32)
    h = jnp.arange(nh)
    for p in range(3):
        e = e.at[p, h, h * LANES + hd + p].set(1.0)
    return e.astype(BF16)


def kernel(x_prompt, x_sample, state_hgrn, cache_sb_k, cache_sb_v, cache_fox_k, cache_fox_v, cache_fox_logf,
           page_table, norm_g, w_in_ab, w_out_ab, lb_param, out_norm_a, w_in_c, b_f_c, w_out_c,
           qk_norm_q, qk_norm_k):
    B, T, D = x_prompt.shape
    Bd = x_sample.shape[0]
    n_ab, _, H_A, DK_A, DV_A = state_hgrn.shape
    n_c = cache_fox_k.shape[0]
    pool, page_size, H_B, HD_B = cache_sb_k.shape[1:]
    H_C, HD_C = cache_fox_k.shape[3:]
    FA, WA, WB, WC = H_A * DK_A, H_A * DV_A, H_B * HD_B, H_C * HD_C
    depth = norm_g.shape[0]
    assert DK_A == DV_A == LANES and HD_B == HD_C == LANES // 2 and page_size == LANES
    assert w_in_ab.shape[2] == 2 * FA + 2 * WA + 4 * WB and w_in_c.shape[2] == 4 * WC + H_C

    tm = min(256, T)
    tc = min(512, T)
    tq_fox = min(FOX_TQ, T)
    assert T % tm == 0 and T % tc == 0 and tc % HGRN_CHUNK == 0 and Bd % HGRN_STEP_SEQS == 0
    assert T % tq_fox == 0 and tq_fox % tm == 0 and H_B % 2 == 0 and H_C % FOX_HEADS == 0
    nt = T // tm

    p = jax.nn.softmax(lb_param.astype(F32), axis=0)
    lower_bounds = jnp.cumsum(p, axis=0) - p[0]

    ltri_chunk = _lower_incl(HGRN_CHUNK)
    ltri_tm = _lower_incl(tm)
    ustrict_tm = _upper_strict(tm)
    bd = _block_diag(256, HD_C)
    emat = _bias_placement(H_C, HD_C)
    n_pages = page_table.shape[1]
    lstrict = _lower_strict(page_size)
    pmat_sb = _later_pages(n_pages, H_B)
    pmat_fox = _later_pages(n_pages, H_C)

    to_t = lambda c: jnp.transpose(c, (0, 1, 3, 4, 2))
    sbk, sbv, fxk, fxv = to_t(cache_sb_k), to_t(cache_sb_v), to_t(cache_fox_k), to_t(cache_fox_v)
    fxlf = jnp.transpose(cache_fox_logf, (0, 1, 3, 2))

    xs = x_sample.reshape(1, Bd, D)
    yp, ys = x_prompt, xs
    hgrn_p, hgrn_s, sbk_p, sbv_p, sbk_s, sbv_s = [], [], [], [], [], []
    fk_p, fv_p, fl_p, fk_s, fv_s, fl_s = [], [], [], [], [], []

    for l in range(depth):
        j = l // 2
        g_pre = norm_g[l].reshape(1, D)
        if l % 2 == 0:
            w_in = w_in_ab[j].astype(BF16)
            w_out = w_out_ab[j].astype(BF16)
            lb = lower_bounds[j].reshape(1, FA)
            g_out = out_norm_a[j].reshape(1, DV_A)
            qa, ka, lf, va, ga, gb, kbo, vbo, qb, kb, vbt = _inproj_ab(yp, g_pre, w_in, lb, fa=FA, wa=WA, wb=WB,
                                                                       hd_b=HD_B, tm=tm)
            oa, S = _hgrn_prompt(qa, ka, va, lf, ltri_chunk, nh=H_A, dk=DK_A, tc=tc)
            obt = _sb_prompt(qb, kb, vbt.reshape(B, nt, H_B, HD_B, tm), ustrict_tm, nh=H_B, hd=HD_B, tq=tm)
            yp = _outproj_ab(yp, oa, ga, obt, gb, g_out, w_out, nh=H_A, dv=DV_A, tm=tm, ob_transposed=True)
            hgrn_p.append(S)
            sbk_p.append(kbo.reshape(B, T, H_B, HD_B))
            sbv_p.append(vbo.reshape(B, T, H_B, HD_B))
            qa, ka, lf, va, ga, gb, kbo, vbo, qb, _, _ = _inproj_ab(ys, g_pre, w_in, lb, fa=FA, wa=WA, wb=WB,
                                                                    hd_b=HD_B, tm=Bd)
            oa, S = _hgrn_step(qa[0], ka[0], va[0], lf[0], state_hgrn[j], nh=H_A, dk=DK_A)
            qd = qb[0].astype(F32).reshape(Bd, 1, WB)
            z = _dec_scores(qd, sbk, j, page_table, nh=H_B, hd=HD_B)
            w = _dec_sb_weights(z, lstrict, pmat_sb)
            ob = _dec_pv(w, sbv, j, page_table, nh=H_B, hd=HD_B)
            ys = _outproj_ab(ys, oa[None], ga, ob.reshape(1, Bd, WB), gb, g_out, w_out, nh=H_A, dv=DV_A, tm=Bd,
                             ob_transposed=False)
            hgrn_s.append(S)
            sbk_s.append(kbo.reshape(Bd, 1, H_B, HD_B))
            sbv_s.append(vbo.reshape(Bd, 1, H_B, HD_B))
        else:
            w_main = w_in_c[j][:, :4 * WC].astype(BF16)
            w_f = jnp.pad(w_in_c[j][:, 4 * WC:], ((0, 0), (0, LANES - H_C))).astype(BF16)
            w_out = w_out_c[j].astype(BF16)
            b_f = b_f_c[j].reshape(1, H_C)
            gq = jnp.tile(qk_norm_q[j], H_C).reshape(1, WC)
            gk = jnp.tile(qk_norm_k[j], H_C).reshape(1, WC)
            qp, kn, v, gate, logf, vt = _inproj_c(yp, g_pre, w_main, w_f, b_f, gq, gk, bd, wc=WC, hd=HD_C, nh=H_C,
                                                  tm=tm, decode=False)
            kp, clast = _fox_prep(kn, logf, ltri_tm, emat, nh=H_C, hd=HD_C, tm=tm)
            bound = (HD_C * (HD_C ** -0.5) * LOG2E * FOX_BOUND_SLACK
                     * jnp.max(jnp.abs(qk_norm_q[j])) * jnp.max(jnp.abs(qk_norm_k[j]))).reshape(1).astype(F32)
            cend = jnp.transpose(clast[:, tq_fox // tm - 1::tq_fox // tm, 0, :], (0, 2, 1))
            ot = _fox_prompt(qp, kp, vt.reshape(B, nt, H_C, HD_C, tm), bound, cend, nh=H_C, hd=HD_C, tq=tq_fox, tm=tm)
            yp = _outproj_c(yp, ot, gate, w_out, tm=tm, o_transposed=True)
            fk_p.append(kn.reshape(B, T, H_C, HD_C))
            fv_p.append(v.reshape(B, T, H_C, HD_C))
            fl_p.append(logf)
            qn, kn, v, gate, logf = _inproj_c(ys, g_pre, w_main, w_f, b_f, gq, gk, bd, wc=WC, hd=HD_C, nh=H_C,
                                              tm=Bd, decode=True)
            qd = qn[0].reshape(Bd, 1, WC)
            kd = kn[0].reshape(Bd, 1, WC)
            vd = v[0].reshape(Bd, 1, WC)
            lfn = logf[0].reshape(Bd, H_C, 1)
            z = _dec_scores(qd * (HD_C ** -0.5), fxk, j, page_table, nh=H_C, hd=HD_C)
            w, wself = _dec_fox_weights(z, qd, kd, lfn, lstrict, pmat_fox, fxlf, j, page_table, nh=H_C, hd=HD_C)
            o = _dec_pv(w, fxv, j, page_table, nh=H_C, hd=HD_C, wself=wself, vnew=vd)
            ys = _outproj_c(ys, o.reshape(1, Bd, WC), gate, w_out, tm=Bd, o_transposed=False)
            fk_s.append(kd.reshape(Bd, 1, H_C, HD_C))
            fv_s.append(vd.reshape(Bd, 1, H_C, HD_C))
            fl_s.append(logf[0].reshape(Bd, 1, H_C))

    return (yp, ys.reshape(Bd, 1, D), jnp.stack(hgrn_p), jnp.stack(hgrn_s), jnp.stack(sbk_p), jnp.stack(sbv_p),
            jnp.stack(sbk_s), jnp.stack(sbv_s), jnp.stack(fk_p), jnp.stack(fv_p), jnp.stack(fl_p),
            jnp.stack(fk_s), jnp.stack(fv_s), jnp.stack(fl_s))
```

```python
import functools

import jax
import jax.numpy as jnp
from jax import lax
from jax.experimental import pallas as pl
from jax.experimental.pallas import tpu as pltpu

F32 = jnp.float32
BF16 = jnp.bfloat16
EPS = 1e-6
NEG_BIG = -1e30
LANES = 128
MIB = 1024 * 1024
NT_DIMS = (((1,), (1,)), ((), ()))


def _cparams(semantics, vmem_mib):
    return pltpu.CompilerParams(dimension_semantics=semantics, vmem_limit_bytes=vmem_mib * MIB)


def _dot(a, b):
    return jnp.dot(a, b, preferred_element_type=F32)


def _dot_nt(a, b):
    return lax.dot_general(a, b, NT_DIMS, preferred_element_type=F32)


def _split2(x):
    hi = x.astype(BF16)
    lo = (x - hi.astype(F32)).astype(BF16)
    return hi, lo


def _split3(x):
    hi = x.astype(BF16)
    r = x - hi.astype(F32)
    mid = r.astype(BF16)
    lo = (r - mid.astype(F32)).astype(BF16)
    return hi, mid, lo


def _dot_left_exact(mat01, x):
    hi, mid, lo = _split3(x)
    return _dot(mat01, hi) + _dot(mat01, mid) + _dot(mat01, lo)


def _sigmoid(x):
    return 1.0 / (1.0 + jnp.exp(-x))


def _softplus(x):
    return jnp.maximum(x, 0.0) + jnp.log(1.0 + jnp.exp(-jnp.abs(x)))


def _rms(x, g):
    return x * lax.rsqrt(jnp.mean(x * x, axis=-1, keepdims=True) + EPS) * g


def _full(shape):
    nd = len(shape)
    return pl.BlockSpec(shape, lambda *_: (0,) * nd)


def _inproj_ab_kernel(x_ref, g_ref, w_ref, lb_ref,
                      qa_ref, ka_ref, lf_ref, va_ref, ga_ref, gb_ref, kbo_ref, vbo_ref,
                      qb_ref, kb_ref, vbt_ref, *, fa, wa, wb, sb_scale):
    xn = _rms(x_ref[...], g_ref[...]).astype(BF16)
    offs = [0]

    def proj(n):
        lo = offs[0]
        offs[0] = lo + n
        return _dot(xn, w_ref[:, lo:lo + n])

    qa = proj(fa)
    za = proj(fa)
    ia = proj(wa)
    ga = proj(wa)
    qb = proj(wb)
    kb = proj(wb)
    vb = proj(wb)
    gb = proj(wb)
    lb = lb_ref[...]
    lf_ref[...] = jnp.log(lb + (1.0 - lb) * _sigmoid(za))
    ka_ref[...] = (1.0 - lb) * _sigmoid(-za)
    qa_ref[...] = qa * _sigmoid(qa)
    va_ref[...] = ia
    ga_ref[...] = ga * _sigmoid(ga)
    gb_ref[...] = gb * _sigmoid(gb)
    kbo_ref[...] = kb
    vbo_ref[...] = vb
    qb_ref[...] = (qb * sb_scale).astype(BF16)
    kb_ref[...] = kb.astype(BF16)
    vbt_ref[...] = vb.T.astype(BF16)


def _inproj_ab(x, g, w_bf16, lb, *, fa, wa, wb, hd_b, tm):
    B, T, D = x.shape
    nt = T // tm
    cols = w_bf16.shape[1]
    row = lambda n: pl.BlockSpec((None, tm, n), lambda b, t: (b, t, 0))
    f32o = lambda n: jax.ShapeDtypeStruct((B, T, n), F32)
    out_shape = (f32o(fa), f32o(fa), f32o(fa), f32o(wa), f32o(wa), f32o(wb), f32o(wb), f32o(wb),
                 jax.ShapeDtypeStruct((B, T, wb), BF16), jax.ShapeDtypeStruct((B, T, wb), BF16),
                 jax.ShapeDtypeStruct((B, nt, wb, tm), BF16))
    out_specs = (row(fa), row(fa), row(fa), row(wa), row(wa), row(wb), row(wb), row(wb), row(wb), row(wb),
                 pl.BlockSpec((None, None, wb, tm), lambda b, t: (b, t, 0, 0)))
    kern = functools.partial(_inproj_ab_kernel, fa=fa, wa=wa, wb=wb, sb_scale=float(hd_b ** -0.5))
    return pl.pallas_call(
        kern, grid=(B, nt),
        in_specs=[row(D), _full((1, D)), _full((D, cols)), _full((1, fa))],
        out_specs=out_specs, out_shape=out_shape,
        compiler_params=_cparams(("parallel", "parallel"), 48), name="inproj_ab",
    )(x, g, w_bf16, lb)


def _outproj_ab_kernel(x_ref, oa_ref, ga_ref, ob_ref, gb_ref, gout_ref, w_ref, y_ref, *, nh, dv, ob_transposed):
    oa = oa_ref[...]
    gout = gout_ref[...]
    parts = [_rms(oa[:, h * dv:(h + 1) * dv], gout) for h in range(nh)]
    oa_n = jnp.concatenate(parts, axis=1) * ga_ref[...]
    ob = ob_ref[...]
    if ob_transposed:
        ob = ob.T
    ob = ob * gb_ref[...]
    cat = jnp.concatenate([oa_n, ob], axis=1).astype(BF16)
    y_ref[...] = x_ref[...] + _dot(cat, w_ref[...])


def _outproj_ab(x, oa, ga, ob, gb, gout, w_bf16, *, nh, dv, tm, ob_transposed):
    B, T, D = x.shape
    nt = T // tm
    wa = oa.shape[-1]
    wb = gb.shape[-1]
    row = lambda n: pl.BlockSpec((None, tm, n), lambda b, t: (b, t, 0))
    if ob_transposed:
        ob_spec = pl.BlockSpec((None, wb, tm), lambda b, t: (b, 0, t))
    else:
        ob_spec = row(wb)
    kern = functools.partial(_outproj_ab_kernel, nh=nh, dv=dv, ob_transposed=ob_transposed)
    return pl.pallas_call(
        kern, grid=(B, nt),
        in_specs=[row(D), row(wa), row(wa), ob_spec, row(wb), _full((1, dv)), _full((wa + wb, D))],
        out_specs=row(D), out_shape=jax.ShapeDtypeStruct((B, T, D), F32),
        compiler_params=_cparams(("parallel", "parallel"), 40), name="outproj_ab",
    )(x, oa, ga, ob, gb, gout, w_bf16)


def _inproj_c_kernel(x_ref, g_ref, w_ref, wf_ref, bf_ref, gq_ref, gk_ref, bd_ref, *out_refs,
                     wc, hd, nh, nb, fox_scale, decode):
    xn = _rms(x_ref[...], g_ref[...]).astype(BF16)
    bd = bd_ref[...]
    inv_hd = 1.0 / hd

    def normed(base, gfull_ref):
        outs = []
        for c in range(wc // nb):
            y = _dot(xn, w_ref[:, base + c * nb: base + (c + 1) * nb])
            hi, lo = _split2(y * y)
            ms = (_dot(hi, bd) + _dot(lo, bd)) * inv_hd
            outs.append(y * lax.rsqrt(ms + EPS) * gfull_ref[:, c * nb:(c + 1) * nb])
        return jnp.concatenate(outs, axis=1)

    qn = normed(0, gq_ref)
    kn = normed(wc, gk_ref)
    v = _dot(xn, w_ref[:, 2 * wc:3 * wc])
    gt = _dot(xn, w_ref[:, 3 * wc:4 * wc])
    f = _dot(xn, wf_ref[...])[:, :nh] + bf_ref[...]
    logf = -_softplus(-f)
    if decode:
        qn_ref, k_ref, v_ref, gate_ref, lf_ref = out_refs
        qn_ref[...] = qn
    else:
        qp_ref, k_ref, v_ref, gate_ref, lf_ref, vt_ref = out_refs
        lane = lax.broadcasted_iota(jnp.int32, (qn.shape[0], LANES), 1)
        ones_cols = jnp.where((lane >= hd) & (lane < hd + 3), 1.0, 0.0)
        for h in range(nh):
            pair = qn[:, (h // 2) * LANES:(h // 2 + 1) * LANES]
            if h % 2 == 1:
                pair = pltpu.roll(pair, hd, axis=1)
            qp_ref[h] = jnp.where(lane < hd, pair * fox_scale, ones_cols).astype(BF16)
        vt_ref[...] = v.T.astype(BF16)
    k_ref[...] = kn
    v_ref[...] = v
    gate_ref[...] = gt * _sigmoid(gt)
    lf_ref[...] = logf


def _inproj_c(x, g, w_bf16, wf_bf16, bf, gq_full, gk_full, bd, *, wc, hd, nh, tm, decode):
    B, T, D = x.shape
    nt = T // tm
    nb = bd.shape[0]
    row = lambda n: pl.BlockSpec((None, tm, n), lambda b, t: (b, t, 0))
    f32o = lambda n: jax.ShapeDtypeStruct((B, T, n), F32)
    common_shapes = (f32o(wc), f32o(wc), f32o(wc), f32o(nh))
    common_specs = (row(wc), row(wc), row(wc), row(nh))
    if decode:
        out_shape = (f32o(wc),) + common_shapes
        out_specs = (row(wc),) + common_specs
    else:
        out_shape = ((jax.ShapeDtypeStruct((B, nh, T, LANES), BF16),) + common_shapes
                     + (jax.ShapeDtypeStruct((B, nt, wc, tm), BF16),))
        out_specs = ((pl.BlockSpec((None, nh, tm, LANES), lambda b, t: (b, 0, t, 0)),) + common_specs
                     + (pl.BlockSpec((None, None, wc, tm), lambda b, t: (b, t, 0, 0)),))
    kern = functools.partial(_inproj_c_kernel, wc=wc, hd=hd, nh=nh, nb=nb, fox_scale=float(hd ** -0.5 * LOG2E), decode=decode)
    return pl.pallas_call(
        kern, grid=(B, nt),
        in_specs=[row(D), _full((1, D)), _full((D, 4 * wc)), _full((D, LANES)), _full((1, nh)),
                  _full((1, wc)), _full((1, wc)), _full((nb, nb))],
        out_specs=out_specs, out_shape=out_shape,
        compiler_params=_cparams(("parallel", "parallel"), 48), name="inproj_c",
    )(x, g, w_bf16, wf_bf16, bf, gq_full, gk_full, bd)


def _outproj_c_kernel(x_ref, o_ref, gate_ref, w_ref, y_ref, *, o_transposed):
    o = o_ref[...]
    if o_transposed:
        o = o.T
    y_ref[...] = x_ref[...] + _dot((o * gate_ref[...]).astype(BF16), w_ref[...])


def _outproj_c(x, o, gate, w_bf16, *, tm, o_transposed):
    B, T, D = x.shape
    nt = T // tm
    wc = gate.shape[-1]
    row = lambda n: pl.BlockSpec((None, tm, n), lambda b, t: (b, t, 0))
    o_spec = pl.BlockSpec((None, wc, tm), lambda b, t: (b, 0, t)) if o_transposed else row(wc)
    kern = functools.partial(_outproj_c_kernel, o_transposed=o_transposed)
    return pl.pallas_call(
        kern, grid=(B, nt),
        in_specs=[row(D), o_spec, row(wc), _full((wc, D))],
        out_specs=row(D), out_shape=jax.ShapeDtypeStruct((B, T, D), F32),
        compiler_params=_cparams(("parallel", "parallel"), 40), name="outproj_c",
    )(x, o, gate, w_bf16)


HGRN_CHUNK = 128
HGRN_SUB = 16
HGRN_EXP_CLAMP = 60.0


def _hgrn_prompt_kernel(q_ref, k_ref, v_ref, g_ref, ltri_ref, o_ref, s_ref, st_ref, *, nh, dk, tc):
    t = pl.program_id(1)

    @pl.when(t == 0)
    def _():
        st_ref[...] = jnp.zeros_like(st_ref)

    C, SB = HGRN_CHUNK, HGRN_SUB
    ltri = ltri_ref[...]

    def chunk(c, carry):
        r0 = pl.multiple_of(c * C, C)
        for h in range(nh):
            sl = slice(h * dk, (h + 1) * dk)
            q = q_ref[pl.ds(r0, C), sl]
            k = k_ref[pl.ds(r0, C), sl]
            v = v_ref[pl.ds(r0, C), sl]
            g = g_ref[pl.ds(r0, C), sl]
            G = _dot_left_exact(ltri, g)
            st = st_ref[h]
            o = _dot_nt((q * jnp.exp(G)).astype(BF16), st.astype(BF16))
            vb = v.astype(BF16)
            outs = []
            for i in range(C // SB):
                a, n = i * SB, (i + 1) * SB
                r = G[a - 1:a, :] if i > 0 else jnp.zeros((1, dk), F32)
                qt = (q[a:n] * jnp.exp(G[a:n] - r)).astype(BF16)
                kt = (k[:n] * jnp.exp(jnp.minimum(r - G[:n], HGRN_EXP_CLAMP))).astype(BF16)
                sc = _dot_nt(qt, kt)
                row = lax.broadcasted_iota(jnp.int32, (SB, n), 0) + a
                col = lax.broadcasted_iota(jnp.int32, (SB, n), 1)
                sc = jnp.where(col <= row, sc, 0.0)
                outs.append(_dot(sc.astype(BF16), vb[:n]))
            o_ref[pl.ds(r0, C), sl] = o + jnp.concatenate(outs, axis=0)
            gend = G[C - 1:C, :]
            kbar = (k * jnp.exp(gend - G)).astype(BF16)
            st_ref[h] = st * jnp.exp(gend) + _dot(v.T.astype(BF16), kbar)
        return carry

    lax.fori_loop(0, tc // C, chunk, 0)

    @pl.when(t == pl.num_programs(1) - 1)
    def _():
        for h in range(nh):
            s_ref[h] = st_ref[h].T


def _hgrn_prompt(q, k, v, g, ltri, *, nh, dk, tc):
    B, T, W = q.shape
    nt = T // tc
    row = pl.BlockSpec((None, tc, W), lambda b, t: (b, t, 0))
    kern = functools.partial(_hgrn_prompt_kernel, nh=nh, dk=dk, tc=tc)
    return pl.pallas_call(
        kern, grid=(B, nt),
        in_specs=[row, row, row, row, _full(ltri.shape)],
        out_specs=(row, pl.BlockSpec((None, nh, dk, dk), lambda b, t: (b, 0, 0, 0))),
        out_shape=(jax.ShapeDtypeStruct((B, T, W), F32), jax.ShapeDtypeStruct((B, nh, dk, dk), F32)),
        scratch_shapes=[pltpu.VMEM((nh, dk, dk), F32)],
        compiler_params=_cparams(("parallel", "arbitrary"), 32), name="hgrn_prompt",
    )(q, k, v, g, ltri)


HGRN_STEP_SEQS = 8


def _hgrn_step_kernel(q_ref, k_ref, v_ref, g_ref, s_ref, o_ref, so_ref, *, nh, dk):
    nb = HGRN_STEP_SEQS
    pad = jnp.zeros((dk - nb, dk), F32)

    def columns(x8):
        return jnp.concatenate([x8, pad], axis=0).T

    for h in range(nh):
        sl = slice(h * dk, (h + 1) * dk)
        egc = columns(jnp.exp(g_ref[:, sl]))
        kc = columns(k_ref[:, sl])
        qc = columns(q_ref[:, sl])
        for j in range(nb):
            s_new = egc[:, j:j + 1] * s_ref[j, h] + kc[:, j:j + 1] * v_ref[j:j + 1, sl]
            so_ref[j, h] = s_new
            o_ref[j:j + 1, sl] = jnp.sum(qc[:, j:j + 1] * s_new, axis=0, keepdims=True)


def _hgrn_step(q, k, v, g, s0, *, nh, dk):
    Bd, W = q.shape
    nb = HGRN_STEP_SEQS
    row = pl.BlockSpec((nb, W), lambda b: (b, 0))
    st = pl.BlockSpec((nb, nh, dk, dk), lambda b: (b, 0, 0, 0))
    kern = functools.partial(_hgrn_step_kernel, nh=nh, dk=dk)
    return pl.pallas_call(
        kern, grid=(Bd // nb,),
        in_specs=[row, row, row, row, st],
        out_specs=(row, st),
        out_shape=(jax.ShapeDtypeStruct((Bd, W), F32), jax.ShapeDtypeStruct(s0.shape, F32)),
        compiler_params=_cparams(("parallel",), 32), name="hgrn_step",
    )(q, k, v, g, s0)


SB_SKIP = 105.0
FOX_SKIP = 152.0
FOX_TQ = 1024
FOX_HEADS = 2
LOG2E = 1.4426950408889634
FOX_BOUND_SLACK = 1.01


def _sb_prompt_kernel(q_ref, k_ref, vt_ref, u_ref, o_ref, *, tq, hd):
    i = pl.program_id(2)
    lane = lax.broadcasted_iota(jnp.int32, (tq, LANES), 1)
    q2 = q_ref[...]
    zero = jnp.zeros_like(q2)
    qs = (jnp.where(lane < hd, q2, zero), jnp.where(lane >= hd, q2, zero))
    u = u_ref[...]
    srow = lax.broadcasted_iota(jnp.int32, (tq, tq), 0)
    tcol = lax.broadcasted_iota(jnp.int32, (tq, tq), 1)
    causal = srow < tcol

    def block(j, state, masked):
        kb = k_ref[pl.ds(pl.multiple_of(j * tq, tq), tq), :]
        out = []
        for g in range(2):
            carry, acc = state[g]
            z = _dot_nt(kb, qs[g])
            sp = _softplus(z)
            if masked:
                sp = jnp.where(causal, sp, 0.0)
            hi, lo = _split2(sp)
            later = _dot(u, hi) + _dot(u, lo)
            w = jnp.exp(z - sp - later - carry)
            if masked:
                w = jnp.where(causal, w, 0.0)
            acc = acc + _dot(vt_ref[j, g], w.astype(BF16))
            carry = carry + jnp.sum(sp, axis=0, keepdims=True)
            out.append((carry, acc))
        return tuple(out)

    init = (jnp.zeros((1, tq), F32), jnp.zeros((hd, tq), F32))
    state = block(i, (init, init), True)

    def cond(st):
        j, state = st
        smallest = jnp.minimum(jnp.min(state[0][0]), jnp.min(state[1][0]))
        return (j >= 0) & (smallest <= SB_SKIP)

    def body(st):
        j, state = st
        return j - 1, block(j, state, False)

    _, state = lax.while_loop(cond, body, (i - 1, state))
    o_ref[...] = jnp.concatenate([state[0][1], state[1][1]], axis=0)


def _sb_prompt(qb, kb, vbt, ustrict, *, nh, hd, tq):
    B, T, W = qb.shape
    nt = T // tq
    kern = functools.partial(_sb_prompt_kernel, tq=tq, hd=hd)
    return pl.pallas_call(
        kern, grid=(B, nh // 2, nt),
        in_specs=[pl.BlockSpec((None, tq, LANES), lambda b, h, i: (b, i, h)),
                  pl.BlockSpec((None, T, LANES), lambda b, h, i: (b, 0, h)),
                  pl.BlockSpec((None, nt, 2, hd, tq), lambda b, h, i: (b, 0, h, 0, 0)),
                  _full((tq, tq))],
        out_specs=pl.BlockSpec((None, 2 * hd, tq), lambda b, h, i: (b, h, i)),
        out_shape=jax.ShapeDtypeStruct((B, W, T), F32),
        compiler_params=_cparams(("parallel", "parallel", "arbitrary"), 40), name="sb_prompt",
    )(qb, kb, vbt, ustrict)


def _fox_prep_kernel(kn_ref, lf_ref, ltri_ref, e_ref, kp_ref, clast_ref, carry_ref, *, nh, hd):
    t = pl.program_id(1)

    @pl.when(t == 0)
    def _():
        carry_ref[...] = jnp.zeros_like(carry_ref)

    c = carry_ref[...] + _dot_left_exact(ltri_ref[...], lf_ref[...])
    tm = c.shape[0]
    carry_ref[...] = c[tm - 1:tm, :]
    c2 = c * LOG2E
    clast_ref[...] = c2[tm - 1:tm, :]
    hi, mid, lo = _split3(-c2)
    aug = _dot(hi, e_ref[0]) + _dot(mid, e_ref[1]) + _dot(lo, e_ref[2])
    kn = kn_ref[...]
    lane = lax.broadcasted_iota(jnp.int32, (tm, LANES), 1)
    for h in range(nh):
        pair = kn[:, (h // 2) * LANES:(h // 2 + 1) * LANES]
        if h % 2 == 1:
            pair = pltpu.roll(pair, hd, axis=1)
        kp_ref[h] = jnp.where(lane < hd, pair, aug[:, h * LANES:(h + 1) * LANES]).astype(BF16)


def _fox_prep(kn, logf, ltri, emat, *, nh, hd, tm):
    B, T, W = kn.shape
    nt = T // tm
    kern = functools.partial(_fox_prep_kernel, nh=nh, hd=hd)
    return pl.pallas_call(
        kern, grid=(B, nt),
        in_specs=[pl.BlockSpec((None, tm, W), lambda b, t: (b, t, 0)),
                  pl.BlockSpec((None, tm, nh), lambda b, t: (b, t, 0)),
                  _full(ltri.shape), _full(emat.shape)],
        out_specs=(pl.BlockSpec((None, nh, tm, LANES), lambda b, t: (b, 0, t, 0)),
                   pl.BlockSpec((None, None, 1, nh), lambda b, t: (b, t, 0, 0))),
        out_shape=(jax.ShapeDtypeStruct((B, nh, T, LANES), BF16), jax.ShapeDtypeStruct((B, nt, 1, nh), F32)),
        scratch_shapes=[pltpu.VMEM((1, nh), F32)],
        compiler_params=_cparams(("parallel", "arbitrary"), 40), name="fox_prep",
    )(kn, logf, ltri, emat)


def _fox_prompt_kernel(bound_ref, cend_ref, q_ref, k_ref, vt_ref, o_ref, *, tq, tm, hd, G):
    b = pl.program_id(0)
    hg = pl.program_id(1)
    i = pl.program_id(2)
    nsub = tq // tm
    srow = lax.broadcasted_iota(jnp.int32, (tq, tq), 0)
    tcol = lax.broadcasted_iota(jnp.int32, (tq, tq), 1)
    causal = srow <= tcol

    def block(j, state, masked):
        out = []
        for g in range(G):
            m, l, acc = state[g]
            s = _dot_nt(k_ref[g, j], q_ref[g])
            if masked:
                s = jnp.where(causal, s, NEG_BIG)
            m_new = jnp.maximum(m, jnp.max(s, axis=0, keepdims=True))
            alpha = jnp.exp2(m - m_new)
            p = jnp.exp2(s - m_new)
            l = alpha * l + jnp.sum(p, axis=0, keepdims=True)
            pb = p.astype(BF16)
            pv = _dot(vt_ref[j * nsub, g], pb[:tm])
            for r in range(1, nsub):
                pv = pv + _dot(vt_ref[j * nsub + r, g], pb[r * tm:(r + 1) * tm])
            out.append((m_new, l, alpha * acc + pv))
        return tuple(out)

    init = (jnp.full((1, tq), NEG_BIG, F32), jnp.zeros((1, tq), F32), jnp.zeros((hd, tq), F32))
    state = block(i, (init,) * G, True)
    bound = bound_ref[0]

    def cond(st):
        j, state = st
        jj = jnp.maximum(j, 0)
        alive = None
        for g in range(G):
            ub = bound - cend_ref[b, hg * G + g, jj]
            a = ub - jnp.min(state[g][0]) > -FOX_SKIP
            alive = a if alive is None else (alive | a)
        return (j >= 0) & alive

    def body(st):
        j, state = st
        return j - 1, block(j, state, False)

    _, state = lax.while_loop(cond, body, (i - 1, state))
    o_ref[...] = jnp.concatenate([acc / l for (_, l, acc) in state], axis=0)


def _fox_prompt(qp, kp, vt, bound, cend, *, nh, hd, tq, tm):
    B, _, T, _ = qp.shape
    nt = T // tq
    G = FOX_HEADS
    kp5 = kp.reshape(B, nh, nt, tq, LANES)
    kern = functools.partial(_fox_prompt_kernel, tq=tq, tm=tm, hd=hd, G=G)
    grid_spec = pltpu.PrefetchScalarGridSpec(
        num_scalar_prefetch=2, grid=(B, nh // G, nt),
        in_specs=[pl.BlockSpec((None, G, tq, LANES), lambda b, h, i, *_: (b, h, i, 0)),
                  pl.BlockSpec((None, G, nt, tq, LANES), lambda b, h, i, *_: (b, h, 0, 0, 0)),
                  pl.BlockSpec((None, T // tm, G, hd, tm), lambda b, h, i, *_: (b, 0, h, 0, 0))],
        out_specs=pl.BlockSpec((None, G * hd, tq), lambda b, h, i, *_: (b, h, i)),
    )
    return pl.pallas_call(
        kern, grid_spec=grid_spec,
        out_shape=jax.ShapeDtypeStruct((B, nh * hd, T), F32),
        compiler_params=_cparams(("parallel", "parallel", "arbitrary"), 48), name="fox_prompt",
    )(bound, cend, qp, kp5, vt)


DEC_PAGES_PER_STEP = 16


def _head_block_mask(nh, hd):
    sub = lax.broadcasted_iota(jnp.int32, (nh, nh * hd), 0)
    lane = lax.broadcasted_iota(jnp.int32, (nh, nh * hd), 1)
    return (lane // hd) == sub


def _dec_scores_kernel(pt_ref, q_ref, *refs, nh, hd, npp):
    k_refs, z_ref = refs[:npp], refs[npp]
    mask = _head_block_mask(nh, hd)
    qblk = jnp.where(mask, jnp.broadcast_to(q_ref[...], mask.shape), 0.0).astype(BF16)
    for p in range(npp):
        kt = k_refs[p][...].reshape(nh * hd, LANES).astype(BF16)
        z_ref[p] = _dot(qblk, kt)


def _page_specs(layer, cache_t, npp):
    blk = (None, None) + tuple(cache_t.shape[2:])

    def mk(p):
        return pl.BlockSpec(blk, lambda b, c, pt: (layer, pt[b, c * npp + p], 0, 0, 0))
    return [mk(p) for p in range(npp)]


def _dec_scores(q, cache_t, layer, page_table, *, nh, hd):
    Bd, n_pages = page_table.shape
    npp = min(DEC_PAGES_PER_STEP, n_pages)
    kern = functools.partial(_dec_scores_kernel, nh=nh, hd=hd, npp=npp)
    grid_spec = pltpu.PrefetchScalarGridSpec(
        num_scalar_prefetch=1, grid=(Bd, n_pages // npp),
        in_specs=[pl.BlockSpec((None, 1, nh * hd), lambda b, c, pt: (b, 0, 0))] + _page_specs(layer, cache_t, npp),
        out_specs=pl.BlockSpec((None, npp, nh, LANES), lambda b, c, pt: (b, c, 0, 0)),
    )
    return pl.pallas_call(
        kern, grid_spec=grid_spec,
        out_shape=jax.ShapeDtypeStruct((Bd, n_pages, nh, LANES), F32),
        compiler_params=_cparams(("parallel", "parallel"), 48), name="dec_scores",
    )(page_table, q, *([cache_t] * npp))


def _dec_pv_kernel(pt_ref, w_ref, *refs, nh, hd, npp, with_self):
    v_refs = refs[:npp]
    rest = refs[npp:]
    if with_self:
        wself_ref, vnew_ref, o_ref, acc_ref = rest
    else:
        o_ref, acc_ref = rest
    c = pl.program_id(1)

    @pl.when(c == 0)
    def _():
        acc_ref[...] = jnp.zeros_like(acc_ref)

    acc = acc_ref[...]
    for p in range(npp):
        vt = v_refs[p][...].reshape(nh * hd, LANES).astype(BF16)
        acc = acc + _dot_nt(w_ref[p].astype(BF16), vt)
    acc_ref[...] = acc

    @pl.when(c == pl.num_programs(1) - 1)
    def _():
        mask = _head_block_mask(nh, hd)
        o = jnp.sum(jnp.where(mask, acc, 0.0), axis=0, keepdims=True)
        if with_self:
            wlane = jnp.sum(jnp.where(mask, jnp.broadcast_to(wself_ref[...], mask.shape), 0.0), axis=0, keepdims=True)
            o = o + wlane * vnew_ref[...]
        o_ref[...] = o


def _dec_pv(w, cache_t, layer, page_table, *, nh, hd, wself=None, vnew=None):
    Bd, n_pages = page_table.shape
    npp = min(DEC_PAGES_PER_STEP, n_pages)
    with_self = wself is not None
    row = pl.BlockSpec((None, 1, nh * hd), lambda b, c, pt: (b, 0, 0))
    in_specs = ([pl.BlockSpec((None, npp, nh, LANES), lambda b, c, pt: (b, c, 0, 0))]
                + _page_specs(layer, cache_t, npp))
    args = [w] + [cache_t] * npp
    if with_self:
        in_specs += [pl.BlockSpec((None, nh, 1), lambda b, c, pt: (b, 0, 0)), row]
        args += [wself, vnew]
    kern = functools.partial(_dec_pv_kernel, nh=nh, hd=hd, npp=npp, with_self=with_self)
    grid_spec = pltpu.PrefetchScalarGridSpec(
        num_scalar_prefetch=1, grid=(Bd, n_pages // npp), in_specs=in_specs, out_specs=row,
        scratch_shapes=[pltpu.VMEM((nh, nh * hd), F32)],
    )
    return pl.pallas_call(
        kern, grid_spec=grid_spec,
        out_shape=jax.ShapeDtypeStruct((Bd, 1, nh * hd), F32),
        compiler_params=_cparams(("parallel", "arbitrary"), 48), name="dec_pv",
    )(page_table, *args)


def _later_tokens(x, lstrict, pmat):
    hi, mid, lo = _split3(x)
    in_page = _dot(hi, lstrict) + _dot(mid, lstrict) + _dot(lo, lstrict)
    page_tot = jnp.broadcast_to(jnp.sum(x, axis=1, keepdims=True), x.shape)
    return in_page + _dot_left_exact(pmat, page_tot)


DEC_WEIGHT_SEQS = 8


def _dec_sb_weights_kernel(z_ref, lstrict_ref, pmat_ref, w_ref, *, n_pages, nh):
    for j in range(DEC_WEIGHT_SEQS):
        z = z_ref[j].reshape(n_pages * nh, LANES)
        sp = _softplus(z)
        later = _later_tokens(sp, lstrict_ref[...], pmat_ref[...])
        w_ref[j] = jnp.exp(z - sp - later).reshape(n_pages, nh, LANES)


def _dec_sb_weights(z, lstrict, pmat):
    Bd, n_pages, nh, _ = z.shape
    nb = DEC_WEIGHT_SEQS
    blk = pl.BlockSpec((nb, n_pages, nh, LANES), lambda b: (b, 0, 0, 0))
    kern = functools.partial(_dec_sb_weights_kernel, n_pages=n_pages, nh=nh)
    return pl.pallas_call(
        kern, grid=(Bd // nb,),
        in_specs=[blk, _full(lstrict.shape), _full(pmat.shape)], out_specs=blk,
        out_shape=jax.ShapeDtypeStruct(z.shape, F32),
        compiler_params=_cparams(("parallel",), 32), name="dec_sb_weights",
    )(z, lstrict, pmat)


def _dec_fox_weights_kernel(pt_ref, z_ref, q_ref, kn_ref, lfn_ref, lstrict_ref, pmat_ref, *refs,
                            nh, hd, n_pages, scale):
    lf_refs = refs[:n_pages]
    w_ref, wself_ref = refs[n_pages:]
    R = n_pages * nh
    lf = jnp.concatenate([r[...] for r in lf_refs], axis=0)
    over_pages = lambda col: jnp.concatenate([col] * n_pages, axis=0)
    decay = _later_tokens(lf, lstrict_ref[...], pmat_ref[...]) + over_pages(lfn_ref[...])
    logits = z_ref[...].reshape(R, LANES) + decay
    mask = _head_block_mask(nh, hd)
    qk = jnp.broadcast_to(q_ref[...] * kn_ref[...], mask.shape)
    s_self = jnp.sum(jnp.where(mask, qk, 0.0), axis=1, keepdims=True) * scale
    row_max = jnp.max(logits, axis=1, keepdims=True)
    m = s_self
    for p in range(n_pages):
        m = jnp.maximum(m, row_max[p * nh:(p + 1) * nh])
    pexp = jnp.exp(logits - over_pages(m))
    p_self = jnp.exp(s_self - m)
    row_sum = jnp.sum(pexp, axis=1, keepdims=True)
    denom = p_self
    for p in range(n_pages):
        denom = denom + row_sum[p * nh:(p + 1) * nh]
    w_ref[...] = (pexp / over_pages(denom)).reshape(n_pages, nh, LANES)
    wself_ref[...] = p_self / denom


def _dec_fox_weights(z, q, kn, lfn, lstrict, pmat, logf_t, layer, page_table, *, nh, hd):
    Bd, n_pages = page_table.shape
    lf_specs = [pl.BlockSpec((None, None, nh, LANES), (lambda p: lambda b, pt: (layer, pt[b, p], 0, 0))(p))
                for p in range(n_pages)]
    kern = functools.partial(_dec_fox_weights_kernel, nh=nh, hd=hd, n_pages=n_pages, scale=float(hd ** -0.5))
    zblk = pl.BlockSpec((None, n_pages, nh, LANES), lambda b, pt: (b, 0, 0, 0))
    row = pl.BlockSpec((None, 1, nh * hd), lambda b, pt: (b, 0, 0))
    col = pl.BlockSpec((None, nh, 1), lambda b, pt: (b, 0, 0))
    grid_spec = pltpu.PrefetchScalarGridSpec(
        num_scalar_prefetch=1, grid=(Bd,),
        in_specs=[zblk, row, row, col,
                  pl.BlockSpec(lstrict.shape, lambda b, pt: (0, 0)),
                  pl.BlockSpec(pmat.shape, lambda b, pt: (0, 0))] + lf_specs,
        out_specs=(zblk, col),
    )
    return pl.pallas_call(
        kern, grid_spec=grid_spec,
        out_shape=(jax.ShapeDtypeStruct(z.shape, F32), jax.ShapeDtypeStruct((Bd, nh, 1), F32)),
        compiler_params=_cparams(("parallel",), 32), name="dec_fox_weights",
    )(page_table, z, q, kn, lfn, lstrict, pmat, *([logf_t] * n_pages))


def _lower_incl(n):
    i = jnp.arange(n)
    return (i[None, :] <= i[:, None]).astype(BF16)


def _upper_strict(n):
    i = jnp.arange(n)
    return (i[None, :] > i[:, None]).astype(BF16)


def _lower_strict(n):
    i = jnp.arange(n)
    return (i[:, None] > i[None, :]).astype(BF16)


def _later_pages(n_pages, nh):
    r = jnp.arange(n_pages * nh)
    return (((r[None, :] % nh) == (r[:, None] % nh)) & (r[None, :] > r[:, None])).astype(BF16)


def _block_diag(n, blk):
    i = jnp.arange(n)
    return ((i[:, None] // blk) == (i[None, :] // blk)).astype(BF16)


def _bias_placement(nh, hd):
    e = jnp.zeros((3, nh, nh * LANES), F32)
    h = jnp.arange(nh)
    for p in range(3):
        e = e.at[p, h, h * LANES + hd + p].set(1.0)
    return e.astype(BF16)


def kernel(x_prompt, x_sample, state_hgrn, cache_sb_k, cache_sb_v, cache_fox_k, cache_fox_v, cache_fox_logf,
           page_table, norm_g, w_in_ab, w_out_ab, lb_param, out_norm_a, w_in_c, b_f_c, w_out_c,
           qk_norm_q, qk_norm_k):
    B, T, D = x_prompt.shape
    Bd = x_sample.shape[0]
    n_ab, _, H_A, DK_A, DV_A = state_hgrn.shape
    n_c = cache_fox_k.shape[0]
    pool, page_size, H_B, HD_B = cache_sb_k.shape[1:]
    H_C, HD_C = cache_fox_k.shape[3:]
    FA, WA, WB, WC = H_A * DK_A, H_A * DV_A, H_B * HD_B, H_C * HD_C
    depth = norm_g.shape[0]
    assert DK_A == DV_A == LANES and HD_B == HD_C == LANES // 2 and page_size == LANES
    assert w_in_ab.shape[2] == 2 * FA + 2 * WA + 4 * WB and w_in_c.shape[2] == 4 * WC + H_C

    tm = min(256, T)
    tc = min(512, T)
    tq_fox = min(FOX_TQ, T)
    assert T % tm == 0 and T % tc == 0 and tc % HGRN_CHUNK == 0 and Bd % HGRN_STEP_SEQS == 0
    assert T % tq_fox == 0 and tq_fox % tm == 0 and H_B % 2 == 0 and H_C % FOX_HEADS == 0
    nt = T // tm

    p = jax.nn.softmax(lb_param.astype(F32), axis=0)
    lower_bounds = jnp.cumsum(p, axis=0) - p[0]

    ltri_chunk = _lower_incl(HGRN_CHUNK)
    ltri_tm = _lower_incl(tm)
    ustrict_tm = _upper_strict(tm)
    bd = _block_diag(256, HD_C)
    emat = _bias_placement(H_C, HD_C)
    n_pages = page_table.shape[1]
    lstrict = _lower_strict(page_size)
    pmat_sb = _later_pages(n_pages, H_B)
    pmat_fox = _later_pages(n_pages, H_C)

    to_t = lambda c: jnp.transpose(c, (0, 1, 3, 4, 2))
    sbk, sbv, fxk, fxv = to_t(cache_sb_k), to_t(cache_sb_v), to_t(cache_fox_k), to_t(cache_fox_v)
    fxlf = jnp.transpose(cache_fox_logf, (0, 1, 3, 2))

    xs = x_sample.reshape(1, Bd, D)
    yp, ys = x_prompt, xs
    hgrn_p, hgrn_s, sbk_p, sbv_p, sbk_s, sbv_s = [], [], [], [], [], []
    fk_p, fv_p, fl_p, fk_s, fv_s, fl_s = [], [], [], [], [], []

    for l in range(depth):
        j = l // 2
        g_pre = norm_g[l].reshape(1, D)
        if l % 2 == 0:
            w_in = w_in_ab[j].astype(BF16)
            w_out = w_out_ab[j].astype(BF16)
            lb = lower_bounds[j].reshape(1, FA)
            g_out = out_norm_a[j].reshape(1, DV_A)
            qa, ka, lf, va, ga, gb, kbo, vbo, qb, kb, vbt = _inproj_ab(yp, g_pre, w_in, lb, fa=FA, wa=WA, wb=WB,
                                                                       hd_b=HD_B, tm=tm)
            oa, S = _hgrn_prompt(qa, ka, va, lf, ltri_chunk, nh=H_A, dk=DK_A, tc=tc)
            obt = _sb_prompt(qb, kb, vbt.reshape(B, nt, H_B, HD_B, tm), ustrict_tm, nh=H_B, hd=HD_B, tq=tm)
            yp = _outproj_ab(yp, oa, ga, obt, gb, g_out, w_out, nh=H_A, dv=DV_A, tm=tm, ob_transposed=True)
            hgrn_p.append(S)
            sbk_p.append(kbo.reshape(B, T, H_B, HD_B))
            sbv_p.append(vbo.reshape(B, T, H_B, HD_B))
            qa, ka, lf, va, ga, gb, kbo, vbo, qb, _, _ = _inproj_ab(ys, g_pre, w_in, lb, fa=FA, wa=WA, wb=WB,
                                                                    hd_b=HD_B, tm=Bd)
            oa, S = _hgrn_step(qa[0], ka[0], va[0], lf[0], state_hgrn[j], nh=H_A, dk=DK_A)
            qd = qb[0].astype(F32).reshape(Bd, 1, WB)
            z = _dec_scores(qd, sbk, j, page_table, nh=H_B, hd=HD_B)
            w = _dec_sb_weights(z, lstrict, pmat_sb)
            ob = _dec_pv(w, sbv, j, page_table, nh=H_B, hd=HD_B)
            ys = _outproj_ab(ys, oa[None], ga, ob.reshape(1, Bd, WB), gb, g_out, w_out, nh=H_A, dv=DV_A, tm=Bd,
                             ob_transposed=False)
            hgrn_s.append(S)
            sbk_s.append(kbo.reshape(Bd, 1, H_B, HD_B))
            sbv_s.append(vbo.reshape(Bd, 1, H_B, HD_B))
        else:
            w_main = w_in_c[j][:, :4 * WC].astype(BF16)
            w_f = jnp.pad(w_in_c[j][:, 4 * WC:], ((0, 0), (0, LANES - H_C))).astype(BF16)
            w_out = w_out_c[j].astype(BF16)
            b_f = b_f_c[j].reshape(1, H_C)
            gq = jnp.tile(qk_norm_q[j], H_C).reshape(1, WC)
            gk = jnp.tile(qk_norm_k[j], H_C).reshape(1, WC)
            qp, kn, v, gate, logf, vt = _inproj_c(yp, g_pre, w_main, w_f, b_f, gq, gk, bd, wc=WC, hd=HD_C, nh=H_C,
                                                  tm=tm, decode=False)
            kp, clast = _fox_prep(kn, logf, ltri_tm, emat, nh=H_C, hd=HD_C, tm=tm)
            bound = (HD_C * (HD_C ** -0.5) * LOG2E * FOX_BOUND_SLACK
                     * jnp.max(jnp.abs(qk_norm_q[j])) * jnp.max(jnp.abs(qk_norm_k[j]))).reshape(1).astype(F32)
            cend = jnp.transpose(clast[:, tq_fox // tm - 1::tq_fox // tm, 0, :], (0, 2, 1))
            ot = _fox_prompt(qp, kp, vt.reshape(B, nt, H_C, HD_C, tm), bound, cend, nh=H_C, hd=HD_C, tq=tq_fox, tm=tm)
            yp = _outproj_c(yp, ot, gate, w_out, tm=tm, o_transposed=True)
            fk_p.append(kn.reshape(B, T, H_C, HD_C))
            fv_p.append(v.reshape(B, T, H_C, HD_C))
            fl_p.append(logf)
            qn, kn, v, gate, logf = _inproj_c(ys, g_pre, w_main, w_f, b_f, gq, gk, bd, wc=WC, hd=HD_C, nh=H_C,
                                              tm=Bd, decode=True)
            qd = qn[0].reshape(Bd, 1, WC)
            kd = kn[0].reshape(Bd, 1, WC)
            vd = v[0].reshape(Bd, 1, WC)
            lfn = logf[0].reshape(Bd, H_C, 1)
            z = _dec_scores(qd * (HD_C ** -0.5), fxk, j, page_table, nh=H_C, hd=HD_C)
            w, wself = _dec_fox_weights(z, qd, kd, lfn, lstrict, pmat_fox, fxlf, j, page_table, nh=H_C, hd=HD_C)
            o = _dec_pv(w, fxv, j, page_table, nh=H_C, hd=HD_C, wself=wself, vnew=vd)
            ys = _outproj_c(ys, o.reshape(1, Bd, WC), gate, w_out, tm=Bd, o_transposed=False)
            fk_s.append(kd.reshape(Bd, 1, H_C, HD_C))
            fv_s.append(vd.reshape(Bd, 1, H_C, HD_C))
            fl_s.append(logf[0].reshape(Bd, 1, H_C))

    return (yp, ys.reshape(Bd, 1, D), jnp.stack(hgrn_p), jnp.stack(hgrn_s), jnp.stack(sbk_p), jnp.stack(sbv_p),
            jnp.stack(sbk_s), jnp.stack(sbv_s), jnp.stack(fk_p), jnp.stack(fv_p), jnp.stack(fl_p),
            jnp.stack(fk_s), jnp.stack(fv_s), jnp.stack(fl_s))
```
